```python
import math
import jax
import jax.numpy as jnp
from jax import lax
import numpy as np

D_MODEL = 1024
BATCH = 32
SEQ = 2048
DEPTH = 4

GRID_W = 64
CTX_LEN = 256
N_MIXERS = 2
CHUNK_A = 128
D_A = 2 * D_MODEL
GROUPS_A = 8
RET_HEADS = 4
RET_DK = D_MODEL // RET_HEADS
RET_DV = 2 * RET_DK
RET_CHUNK = 128
QK_W = RET_HEADS * RET_DK
V_W = RET_HEADS * RET_DV
B_IN_W = 2 * QK_W + 3 * V_W
ROPE_BASE = 10000.0
D_FF = 4 * D_MODEL
LN_EPS = 1e-5
DN_ALPHA = (2 * DEPTH) ** 0.25
DN_BETA = (8 * DEPTH) ** -0.25
N_A_LAYERS = (DEPTH + 1) // 2
N_B_LAYERS = DEPTH // 2

kernel_name = 'hybrid_gmlp_retention_deepnorm_dit'


def _layernorm(x, g, b):
    x32 = x.astype(jnp.float32)
    mu = jnp.mean(x32, -1, keepdims=True)
    var = jnp.mean(jnp.square(x32 - mu), -1, keepdims=True)
    return ((x32 - mu) * lax.rsqrt(var + LN_EPS)).astype(x.dtype) * g + b


def _headnorm(y):
    mu = jnp.mean(y, -1, keepdims=True)
    var = jnp.mean(jnp.square(y - mu), -1, keepdims=True)
    return (y - mu) * lax.rsqrt(var + LN_EPS)


def _sq_relu_mlp(h, w1, w2):
    return jnp.square(jax.nn.relu(h @ w1)) @ w2


def _chunk_gmlp(h, w_in, b_in, ln_g, ln_b, w_s, b_s, w_out):
    bn, L, _ = h.shape
    z = jax.nn.gelu(h @ w_in + b_in, approximate=False)
    u, v = z[..., :D_A], z[..., D_A:]
    v = _layernorm(v, ln_g, ln_b)
    v = v.reshape(bn, L // CHUNK_A, CHUNK_A, GROUPS_A, D_A // GROUPS_A)
    s = jnp.einsum('gij,bnjgc->bnigc', w_s, v) + b_s.T[:, :, None]
    return (u * s.reshape(bn, L, D_A)) @ w_out


def _rope_2d_tables(L, dtype):
    t = jnp.arange(L)
    row = (t // GRID_W).astype(jnp.float32)
    col = (t % GRID_W).astype(jnp.float32)
    n_freq = RET_DK // 4
    inv = ROPE_BASE ** (-jnp.arange(n_freq, dtype=jnp.float32) / n_freq)
    ang = jnp.concatenate([row[:, None] * inv, col[:, None] * inv], -1)
    return jnp.cos(ang).astype(dtype), jnp.sin(ang).astype(dtype)


def _rope(t, cos, sin):
    half = t.shape[-1] // 2
    t1, t2 = t[..., :half], t[..., half:]
    cs, sn = cos[None, :, None, :], sin[None, :, None, :]
    return jnp.concatenate([t1 * cs - t2 * sn, t1 * sn + t2 * cs], -1)


def _retention_scan(q, k, v, log_gamma, s0):
    bn, L, H, _ = q.shape
    dv = v.shape[-1]
    C = RET_CHUNK
    n = L // C
    pos = jnp.arange(C, dtype=jnp.float32)
    diff = pos[:, None] - pos[None, :]
    decay_in = jnp.where(diff[None] >= 0, jnp.exp(log_gamma[:, None, None] * jnp.maximum(diff, 0.0)[None]), 0.0)
    xi = jnp.exp(log_gamma[None, :] * (pos[:, None] + 1.0))
    zeta = jnp.exp(log_gamma[None, :] * (C - 1.0 - pos[:, None]))
    gamma_c = jnp.exp(log_gamma * C)

    def to_chunks(t):
        return t.astype(jnp.float32).reshape(bn, n, C, H, t.shape[-1]).swapaxes(0, 1)

    def step(S, qkv):
        qc, kc, vc = qkv
        scores = jnp.einsum('bihd,bjhd->bhij', qc, kc) * decay_in[None]
        intra = jnp.einsum('bhij,bjhe->bihe', scores, vc)
        inter = jnp.einsum('bihd,bhde->bihe', qc, S) * xi[None, :, :, None]
        S = S * gamma_c[None, :, None, None] + jnp.einsum('bjhd,bjhe->bhde', kc * zeta[None, :, :, None], vc)
        return S, intra + inter

    S, out = lax.scan(step, s0, (to_chunks(q), to_chunks(k), to_chunks(v)))
    return out.swapaxes(0, 1).reshape(bn, L, H, dv), S


def _retention_out(o_f, o_b, g_f, g_b, w_out):
    y = _headnorm(o_f) * jax.nn.silu(g_f.astype(jnp.float32)) + _headnorm(o_b) * jax.nn.silu(g_b.astype(jnp.float32))
    bn, L = y.shape[:2]
    return y.reshape(bn, L, V_W).astype(w_out.dtype) @ w_out


def _split_heads(p, bn, L):
    q = p[..., :QK_W].reshape(bn, L, RET_HEADS, RET_DK)
    k = p[..., QK_W:2 * QK_W].reshape(bn, L, RET_HEADS, RET_DK) * (RET_DK ** -0.5)
    v = p[..., 2 * QK_W:2 * QK_W + V_W].reshape(bn, L, RET_HEADS, RET_DV)
    return q, k, v


def _gates(p, bn, L):
    o = 2 * QK_W + V_W
    g_f = p[..., o:o + V_W].reshape(bn, L, RET_HEADS, RET_DV)
    g_b = p[..., o + V_W:o + 2 * V_W].reshape(bn, L, RET_HEADS, RET_DV)
    return g_f, g_b


def _retention_mixer(h_l, h_c, w_in, decay_p, w_out, need_ctx_out):
    bn, L, _ = h_l.shape
    lc = h_c.shape[1]
    log_gamma = jnp.log1p(-jnp.exp(decay_p.astype(jnp.float32)))
    flip = lambda t: jnp.flip(t, 1)

    p_c = h_c @ (w_in if need_ctx_out else w_in[:, :2 * QK_W + V_W])
    q_c, k_c, v_c = _split_heads(p_c, bn, lc)
    s0 = jnp.zeros((bn, RET_HEADS, RET_DK, RET_DV), jnp.float32)
    of_c, s_f = _retention_scan(q_c, k_c, v_c, log_gamma[0], s0)
    ob_c, s_b = _retention_scan(flip(q_c), flip(k_c), flip(v_c), log_gamma[1], s0)

    p_l = h_l @ w_in
    q_l, k_l, v_l = _split_heads(p_l, bn, L)
    cos, sin = _rope_2d_tables(L, q_l.dtype)
    q_l, k_l = _rope(q_l, cos, sin), _rope(k_l, cos, sin)
    of_l, _ = _retention_scan(q_l, k_l, v_l, log_gamma[0], s_f)
    ob_l, _ = _retention_scan(flip(q_l), flip(k_l), flip(v_l), log_gamma[1], s_b)
    gf_l, gb_l = _gates(p_l, bn, L)
    y_l = _retention_out(of_l, flip(ob_l), gf_l, gb_l, w_out)
    if need_ctx_out:
        gf_c, gb_c = _gates(p_c, bn, lc)
        y_c = _retention_out(of_c, flip(ob_c), gf_c, gb_c, w_out)
    else:
        y_c = None
    return y_l, y_c


def setup_inputs(seed: int = 0) -> dict:
    key = jax.random.key(seed)
    ks = jax.random.split(key, 26)
    f32 = jnp.float32

    def nrm(k, shape, scale=1.0):
        return jax.random.normal(k, shape, f32) * scale

    h_idx = jnp.arange(RET_HEADS, dtype=f32)
    decay_base = -(5.0 + h_idx) * jnp.log(2.0)
    return {
        'x': nrm(ks[0], (BATCH, SEQ, D_MODEL)),
        'c': nrm(ks[1], (BATCH, D_MODEL)),
        'ctx': nrm(ks[2], (BATCH, CTX_LEN, D_MODEL)),
        'c_ctx': nrm(ks[3], (D_MODEL,)),
        'ada_w': nrm(ks[4], (DEPTH, D_MODEL, 6 * D_MODEL), D_MODEL ** -0.5),
        'ada_b': nrm(ks[5], (DEPTH, 6 * D_MODEL), 0.02),
        'ln1_g': 1.0 + nrm(ks[6], (DEPTH, D_MODEL), 0.02),
        'ln1_b': nrm(ks[7], (DEPTH, D_MODEL), 0.02),
        'ln2_g': 1.0 + nrm(ks[8], (DEPTH, D_MODEL), 0.02),
        'ln2_b': nrm(ks[9], (DEPTH, D_MODEL), 0.02),
        'ffn_w1': nrm(ks[10], (DEPTH, D_MODEL, D_FF), D_MODEL ** -0.5),
        'ffn_w2': nrm(ks[11], (DEPTH, D_FF, D_MODEL), D_FF ** -0.5 * DN_BETA),
        'a_w_in': nrm(ks[12], (N_A_LAYERS, D_MODEL, 2 * D_A), D_MODEL ** -0.5),
        'a_b_in': nrm(ks[13], (N_A_LAYERS, 2 * D_A), 0.02),
        'a_ln_g': 1.0 + nrm(ks[14], (N_A_LAYERS, D_A), 0.02),
        'a_ln_b': nrm(ks[15], (N_A_LAYERS, D_A), 0.02),
        'a_w_s': nrm(ks[16], (N_A_LAYERS, GROUPS_A, CHUNK_A, CHUNK_A), CHUNK_A ** -0.5),
        'a_b_s': 1.0 + nrm(ks[17], (N_A_LAYERS, GROUPS_A, CHUNK_A), 0.02),
        'a_w_out': nrm(ks[18], (N_A_LAYERS, D_A, D_MODEL), D_A ** -0.5 * DN_BETA),
        'b_w_in': nrm(ks[19], (N_B_LAYERS, D_MODEL, B_IN_W), D_MODEL ** -0.5),
        'b_decay': jnp.broadcast_to(decay_base, (N_B_LAYERS, 2, RET_HEADS)) + nrm(ks[20], (N_B_LAYERS, 2, RET_HEADS), 0.05),
        'b_w_out': nrm(ks[21], (N_B_LAYERS, V_W, D_MODEL), V_W ** -0.5 * DN_BETA),
    }


def reference(x, c, ctx, c_ctx, ada_w, ada_b, ln1_g, ln1_b, ln2_g, ln2_b, ffn_w1, ffn_w2,
              a_w_in, a_b_in, a_ln_g, a_ln_b, a_w_s, a_b_s, a_w_out, b_w_in, b_decay, b_w_out):
    sc = jax.nn.silu(c)
    scc = jax.nn.silu(c_ctx)
    for i in range(DEPTH):
        need_ctx_out = i < DEPTH - 1
        mod_l = sc @ ada_w[i] + ada_b[i]
        sh1, s1, g1, sh2, s2, g2 = jnp.split(mod_l[:, None, :], 6, axis=-1)
        mod_c = scc @ ada_w[i] + ada_b[i]
        sh1c, s1c, g1c, sh2c, s2c, g2c = jnp.split(mod_c, 6, axis=-1)
        j = i // N_MIXERS
        h_l = x * (1.0 + s1) + sh1
        if i % N_MIXERS == 0:
            y_l = _chunk_gmlp(h_l, a_w_in[j], a_b_in[j], a_ln_g[j], a_ln_b[j], a_w_s[j], a_b_s[j], a_w_out[j])
            if need_ctx_out:
                y_c = _chunk_gmlp(ctx * (1.0 + s1c) + sh1c, a_w_in[j], a_b_in[j], a_ln_g[j], a_ln_b[j],
                                  a_w_s[j], a_b_s[j], a_w_out[j])
        else:
            y_l, y_c = _retention_mixer(h_l, ctx * (1.0 + s1c) + sh1c, b_w_in[j], b_decay[j], b_w_out[j],
                                        need_ctx_out)
        x = _layernorm(DN_ALPHA * x + g1 * y_l, ln1_g[i], ln1_b[i])
        x = _layernorm(DN_ALPHA * x + g2 * _sq_relu_mlp(x * (1.0 + s2) + sh2, ffn_w1[i], ffn_w2[i]),
                       ln2_g[i], ln2_b[i])
        if need_ctx_out:
            ctx = _layernorm(DN_ALPHA * ctx + g1c * y_c, ln1_g[i], ln1_b[i])
            ctx = _layernorm(DN_ALPHA * ctx + g2c * _sq_relu_mlp(ctx * (1.0 + s2c) + sh2c, ffn_w1[i], ffn_w2[i]),
                             ln2_g[i], ln2_b[i])
    return x
```

```python
import functools
import math

import jax
import jax.numpy as jnp
from jax import lax
from jax.experimental import pallas as pl
from jax.experimental.pallas import tpu as pltpu

F32 = jnp.float32
BF16 = jnp.bfloat16

LN_EPS = 1e-5
GRID_W = 64
ROPE_BASE = 10000.0
RET_CHUNK = 256
ROW_TILE = 256
FFN_COL_CHUNK = 1024
VMEM_LIMIT = 56 * 1024 * 1024


def _params(n_axes, vmem=VMEM_LIMIT):
    return pltpu.CompilerParams(dimension_semantics=("arbitrary",) * n_axes, vmem_limit_bytes=vmem)


def _const_spec(shape):
    zeros = (0,) * len(shape)
    return pl.BlockSpec(shape, lambda *_: zeros, pipeline_mode=pl.Buffered(1))


def _mod_spec(d, tiles_per_row):
    return pl.BlockSpec((None, 6, d), lambda t: (t // tiles_per_row, 0, 0))


def _layernorm(y, g, b):
    mu = jnp.mean(y, axis=-1, keepdims=True)
    yc = y - mu
    var = jnp.mean(yc * yc, axis=-1, keepdims=True)
    return yc * lax.rsqrt(var + LN_EPS) * g + b


def _modulate(x, shift, scale):
    return (x * (1.0 + scale) + shift).astype(BF16)


def _gelu(z):
    return 0.5 * z * (1.0 + lax.erf(z * math.sqrt(0.5)))


def _modulation_kernel(c_ref, w_ref, b_ref, o_ref):
    sc = c_ref[...]
    sc = sc * jax.nn.sigmoid(sc)
    o_ref[...] = jnp.dot(sc, w_ref[...], preferred_element_type=F32,
                         precision=lax.Precision.HIGHEST) + b_ref[...]


def _modulation(cc, ada_w, ada_b):
    depth, d, n = ada_w.shape
    nb = cc.shape[0]
    tn = 1536
    return pl.pallas_call(
        _modulation_kernel,
        grid=(depth, n // tn),
        in_specs=[
            pl.BlockSpec((nb, d), lambda i, j: (0, 0)),
            pl.BlockSpec((None, d, tn), lambda i, j: (i, 0, j)),
            pl.BlockSpec((None, 1, tn), lambda i, j: (i, 0, j)),
        ],
        out_specs=pl.BlockSpec((None, nb, tn), lambda i, j: (i, 0, j)),
        out_shape=jax.ShapeDtypeStruct((depth, nb, n), F32),
        compiler_params=_params(2),
        name="modulation",
    )(cc, ada_w, ada_b.reshape(depth, 1, n))


def _ffn_kernel(x_ref, mod_ref, w1_ref, w2_ref, g_ref, b_ref, o_ref, *, alpha):
    x = x_ref[...]
    h = _modulate(x, mod_ref[3:4, :], mod_ref[4:5, :])
    n_hidden = w1_ref.shape[1]
    acc = None
    for c0 in range(0, n_hidden, FFN_COL_CHUNK):
        z = jnp.dot(h, w1_ref[:, c0:c0 + FFN_COL_CHUNK], preferred_element_type=F32)
        a = jnp.maximum(z, 0.0)
        a = (a * a).astype(BF16)
        part = jnp.dot(a, w2_ref[c0:c0 + FFN_COL_CHUNK, :], preferred_element_type=F32)
        acc = part if acc is None else acc + part
    y = alpha * x + mod_ref[5:6, :] * acc
    o_ref[...] = _layernorm(y, g_ref[...], b_ref[...])


def _ffn(x, mod, w1, w2, ln_g, ln_b, *, alpha, tm):
    rows, d = x.shape
    f = w1.shape[1]
    tiles_per_row = rows // mod.shape[0] // tm
    return pl.pallas_call(
        functools.partial(_ffn_kernel, alpha=alpha),
        grid=(rows // tm,),
        in_specs=[
            pl.BlockSpec((tm, d), lambda t: (t, 0)),
            _mod_spec(d, tiles_per_row),
            _const_spec((d, f)),
            _const_spec((f, d)),
            _const_spec((1, d)),
            _const_spec((1, d)),
        ],
        out_specs=pl.BlockSpec((tm, d), lambda t: (t, 0)),
        out_shape=jax.ShapeDtypeStruct((rows, d), F32),
        compiler_params=_params(1),
        name="ffn",
    )(x, mod, w1, w2, ln_g, ln_b)


def _gmlp_kernel(x_ref, mod_ref, win_ref, bin_ref, lng_ref, lnb_ref, ws_ref, bs_ref, wout_ref,
                 g_ref, b_ref, o_ref, s_ref, *, alpha):
    tm = x_ref.shape[0]
    d_a = wout_ref.shape[0]
    n_groups, chunk, _ = ws_ref.shape
    cg = d_a // n_groups
    x = x_ref[...]
    h = _modulate(x, mod_ref[0:1, :], mod_ref[1:2, :])

    v = jnp.dot(h, win_ref[:, d_a:], preferred_element_type=F32) + bin_ref[:, d_a:]
    v = _layernorm(_gelu(v), lng_ref[...], lnb_ref[...]).astype(BF16)
    for r0 in range(0, tm, chunk):
        for g in range(n_groups):
            s_ref[r0:r0 + chunk, g * cg:(g + 1) * cg] = (
                jnp.dot(ws_ref[g], v[r0:r0 + chunk, g * cg:(g + 1) * cg], preferred_element_type=F32)
                + bs_ref[:, g * cg:(g + 1) * cg])

    ck = 1024
    acc = None
    for c0 in range(0, d_a, ck):
        u = _gelu(jnp.dot(h, win_ref[:, c0:c0 + ck], preferred_element_type=F32) + bin_ref[:, c0:c0 + ck])
        gated = (u * s_ref[:, c0:c0 + ck]).astype(BF16)
        part = jnp.dot(gated, wout_ref[c0:c0 + ck, :], preferred_element_type=F32)
        acc = part if acc is None else acc + part
    y = alpha * x + mod_ref[2:3, :] * acc
    o_ref[...] = _layernorm(y, g_ref[...], b_ref[...])


def _gmlp(x, mod, w_in, b_in, ln_g, ln_b, w_s, b_s_rows, w_out, ln1_g, ln1_b, *, alpha, tm):
    rows, d = x.shape
    d_a = w_out.shape[0]
    n_groups, chunk, _ = w_s.shape
    tiles_per_row = rows // mod.shape[0] // tm
    return pl.pallas_call(
        functools.partial(_gmlp_kernel, alpha=alpha),
        grid=(rows // tm,),
        in_specs=[
            pl.BlockSpec((tm, d), lambda t: (t, 0)),
            _mod_spec(d, tiles_per_row),
            _const_spec((d, 2 * d_a)),
            _const_spec((1, 2 * d_a)),
            _const_spec((1, d_a)),
            _const_spec((1, d_a)),
            _const_spec((n_groups, chunk, chunk)),
            _const_spec((chunk, d_a)),
            _const_spec((d_a, d)),
            _const_spec((1, d)),
            _const_spec((1, d)),
        ],
        out_specs=pl.BlockSpec((tm, d), lambda t: (t, 0)),
        out_shape=jax.ShapeDtypeStruct((rows, d), F32),
        scratch_shapes=[pltpu.VMEM((tm, d_a), F32)],
        compiler_params=_params(1),
        name="gmlp",
    )(x, mod, w_in, b_in, ln_g, ln_b, w_s, b_s_rows, w_out, ln1_g, ln1_b)


def _rope_pairs(t1, t2, cos, sin):
    return t1 * cos - t2 * sin, t1 * sin + t2 * cos


def _ret_proj_kernel(*refs, rope, n_heads):
    if rope:
        (x_ref, mod_ref, wq_ref, wkt_ref, wr_ref, cos_ref, sin_ref, cost_ref, sint_ref,
         q_ref, kt_ref, v_ref, gf_ref, gb_ref) = refs
    else:
        x_ref, mod_ref, wq_ref, wkt_ref, wr_ref, q_ref, kt_ref, v_ref, gf_ref, gb_ref = refs
    qk_w = wq_ref.shape[1]
    dk = qk_w // n_heads
    half = dk // 2
    v_w = v_ref.shape[1]
    h = _modulate(x_ref[...], mod_ref[0:1, :], mod_ref[1:2, :])

    q = jnp.dot(h, wq_ref[...], preferred_element_type=F32)
    kt = lax.dot_general(wkt_ref[...], h, (((1,), (1,)), ((), ())), preferred_element_type=F32)
    kt = kt * (dk ** -0.5)
    if rope:
        cos, sin = cos_ref[...], sin_ref[...]
        cost, sint = cost_ref[...], sint_ref[...]
        for hh in range(n_heads):
            a, b = hh * dk, hh * dk + half
            o1, o2 = _rope_pairs(q[:, a:b], q[:, b:b + half], cos, sin)
            q_ref[:, a:b] = o1.astype(BF16)
            q_ref[:, b:b + half] = o2.astype(BF16)
            o1, o2 = _rope_pairs(kt[a:b, :], kt[b:b + half, :], cost, sint)
            kt_ref[a:b, :] = o1.astype(BF16)
            kt_ref[b:b + half, :] = o2.astype(BF16)
    else:
        q_ref[...] = q.astype(BF16)
        kt_ref[...] = kt.astype(BF16)
    for j, ref in enumerate((v_ref, gf_ref, gb_ref)):
        ref[...] = jnp.dot(h, wr_ref[:, j * v_w:(j + 1) * v_w], preferred_element_type=F32).astype(BF16)


def _ret_proj(x, mod, wq, wkt, wr, rope_tabs, *, n_batch, n_heads):
    rows, d = x.shape
    tm = RET_CHUNK
    seq = rows // n_batch
    npb = seq // tm
    qk_w = wq.shape[1]
    v_w = wr.shape[1] // 3
    tiles_per_row = rows // mod.shape[0] // tm
    in_specs = [
        pl.BlockSpec((tm, d), lambda t: (t, 0)),
        _mod_spec(d, tiles_per_row),
        _const_spec((d, qk_w)),
        _const_spec((qk_w, d)),
        _const_spec((d, 3 * v_w)),
    ]
    args = [x, mod, wq, wkt, wr]
    rope = rope_tabs is not None
    if rope:
        half = qk_w // n_heads // 2
        in_specs += [
            pl.BlockSpec((tm, half), lambda t: (t % npb, 0)),
            pl.BlockSpec((tm, half), lambda t: (t % npb, 0)),
            pl.BlockSpec((half, tm), lambda t: (0, t % npb)),
            pl.BlockSpec((half, tm), lambda t: (0, t % npb)),
        ]
        args += list(rope_tabs)
    row_spec = lambda w: pl.BlockSpec((tm, w), lambda t: (t, 0))
    return pl.pallas_call(
        functools.partial(_ret_proj_kernel, rope=rope, n_heads=n_heads),
        grid=(rows // tm,),
        in_specs=in_specs,
        out_specs=[
            row_spec(qk_w),
            pl.BlockSpec((None, None, qk_w, tm), lambda t: (t // npb, t % npb, 0, 0)),
            row_spec(v_w), row_spec(v_w), row_spec(v_w),
        ],
        out_shape=[
            jax.ShapeDtypeStruct((rows, qk_w), BF16),
            jax.ShapeDtypeStruct((n_batch, npb, qk_w, tm), BF16),
            jax.ShapeDtypeStruct((rows, v_w), BF16),
            jax.ShapeDtypeStruct((rows, v_w), BF16),
            jax.ShapeDtypeStruct((rows, v_w), BF16),
        ],
        compiler_params=_params(1),
        name="ret_proj_rope" if rope else "ret_proj",
    )(*args)


def _ret_scan_kernel(*refs, ctx_out):
    (dec_ref, qc_ref, kc_ref, vc_ref, gfc_ref, gbc_ref, ql_ref, kl_ref, vl_ref, gfl_ref, gbl_ref) = refs[:11]
    if ctx_out:
        yl_ref, yc_ref = refs[11:13]
        scratch = refs[13:]
    else:
        yl_ref, yc_ref = refs[11], None
        scratch = refs[12:]
    mask_ref, xi_ref, zeta_ref, s_ref, yacc_ref = scratch
    n_c, dk, c = kc_ref.shape
    n_l = kl_ref.shape[0]
    lc = n_c * c

    def log_gamma():
        return jnp.log1p(-jnp.exp(dec_ref[...]))

    @pl.when(pl.program_id(1) == 0)
    def _():
        lg = log_gamma()
        lgf, lgb = lg[0:1, :], lg[1:2, :]
        ii = lax.broadcasted_iota(jnp.int32, (c, c), 0).astype(F32)
        jj = lax.broadcasted_iota(jnp.int32, (c, c), 1).astype(F32)
        diff = ii - jj
        mask_ref[0] = jnp.where(diff >= 0, jnp.exp(lgf * jnp.maximum(diff, 0.0)), 0.0)
        mask_ref[1] = jnp.where(diff <= 0, jnp.exp(lgb * jnp.maximum(-diff, 0.0)), 0.0)
        pi = lax.broadcasted_iota(jnp.int32, (c, dk), 0).astype(F32)
        xi_ref[0] = jnp.exp(lgf * (pi + 1.0))
        xi_ref[1] = jnp.exp(lgb * (c - pi))
        pj = lax.broadcasted_iota(jnp.int32, (8, c), 1).astype(F32)
        zeta_ref[0] = jnp.exp(lgf * (c - 1.0 - pj))
        zeta_ref[1] = jnp.exp(lgb * pj)

    gamma_c = jnp.exp(log_gamma() * float(c))

    def chunk_step(q_ref, k_ref, v_ref, g_ref, ci, d, row_off, emit):
        r0 = ci * c if isinstance(ci, int) else pl.multiple_of(ci * c, c)
        qc = q_ref[pl.ds(r0, c), :]
        kt = k_ref[ci]
        vc = v_ref[pl.ds(r0, c), :]
        state = s_ref[d]
        kz = (kt.astype(F32) * zeta_ref[d, 0:1, :]).astype(BF16)
        s_ref[d] = state * gamma_c[d:d + 1, :] + jnp.dot(kz, vc, preferred_element_type=F32)
        if not emit:
            return
        scores = jnp.dot(qc, kt, preferred_element_type=F32)
        p = (scores * mask_ref[d]).astype(BF16)
        qx = (qc.astype(F32) * xi_ref[d]).astype(BF16)
        o = (jnp.dot(p, vc, preferred_element_type=F32)
             + jnp.dot(qx, state.astype(BF16), preferred_element_type=F32))
        mu = jnp.mean(o, axis=-1, keepdims=True)
        oc = o - mu
        var = jnp.mean(oc * oc, axis=-1, keepdims=True)
        gate = g_ref[pl.ds(r0, c), :].astype(F32)
        rows = pl.ds(row_off + r0, c)
        yacc_ref[rows, :] += oc * lax.rsqrt(var + LN_EPS) * (gate * jax.nn.sigmoid(gate))

    s_ref[...] = jnp.zeros_like(s_ref)
    yacc_ref[...] = jnp.zeros_like(yacc_ref)
    for ci in range(n_c):
        chunk_step(qc_ref, kc_ref, vc_ref, gfc_ref, ci, 0, 0, ctx_out)
        chunk_step(qc_ref, kc_ref, vc_ref, gbc_ref, n_c - 1 - ci, 1, 0, ctx_out)

    def body(i, carry):
        chunk_step(ql_ref, kl_ref, vl_ref, gfl_ref, i, 0, lc, True)
        chunk_step(ql_ref, kl_ref, vl_ref, gbl_ref, n_l - 1 - i, 1, lc, True)
        return carry

    lax.fori_loop(0, n_l, body, 0)

    def writeback(i, carry):
        r0 = pl.multiple_of(i * c, c)
        yl_ref[pl.ds(r0, c), :] = yacc_ref[pl.ds(lc + r0, c), :].astype(BF16)
        return carry

    lax.fori_loop(0, n_l, writeback, 0)
    if ctx_out:
        yc_ref[...] = yacc_ref[0:lc, :].astype(BF16)


def _ret_scan(decay, ctx_parts, lat_parts, *, n_batch, n_heads, ctx_out):
    qc, kc, vc, gfc, gbc = ctx_parts
    ql, kl, vl, gfl, gbl = lat_parts
    _, n_c, qk_w, c = kc.shape
    n_l = kl.shape[1]
    dk = qk_w // n_heads
    dv = vl.shape[1] // n_heads
    lc, ll = n_c * c, n_l * c

    def seq_specs(seq, n_chunks):
        row = lambda w: pl.BlockSpec((seq, w), lambda h, b: (b, h))
        return [row(dk), pl.BlockSpec((None, n_chunks, dk, c), lambda h, b: (b, 0, h, 0)),
                row(dv), row(dv), row(dv)]

    out_specs = [pl.BlockSpec((ll, dv), lambda h, b: (b, h))]
    out_shape = [jax.ShapeDtypeStruct((n_batch * ll, n_heads * dv), BF16)]
    if ctx_out:
        out_specs.append(pl.BlockSpec((lc, dv), lambda h, b: (b, h)))
        out_shape.append(jax.ShapeDtypeStruct((n_batch * lc, n_heads * dv), BF16))
    res = pl.pallas_call(
        functools.partial(_ret_scan_kernel, ctx_out=ctx_out),
        grid=(n_heads, n_batch),
        in_specs=[pl.BlockSpec((None, 2, 1), lambda h, b: (h, 0, 0))] + seq_specs(lc, n_c) + seq_specs(ll, n_l),
        out_specs=out_specs,
        out_shape=out_shape,
        scratch_shapes=[
            pltpu.VMEM((2, c, c), F32),
            pltpu.VMEM((2, c, dk), F32),
            pltpu.VMEM((2, 8, c), F32),
            pltpu.VMEM((2, dk, dv), F32),
            pltpu.VMEM((lc + ll, dv), F32),
        ],
        compiler_params=_params(2),
        name="ret_scan",
    )(decay, qc, kc, vc, gfc, gbc, ql, kl, vl, gfl, gbl)
    return res if ctx_out else (res[0], None)


def _ret_out_kernel(y_ref, x_ref, mod_ref, w_ref, g_ref, b_ref, o_ref, *, alpha):
    acc = jnp.dot(y_ref[...], w_ref[...], preferred_element_type=F32)
    y = alpha * x_ref[...] + mod_ref[2:3, :] * acc
    o_ref[...] = _layernorm(y, g_ref[...], b_ref[...])


def _ret_out(y, x, mod, w_out, ln_g, ln_b, *, alpha, tm):
    rows, d = x.shape
    v_w = y.shape[1]
    tiles_per_row = rows // mod.shape[0] // tm
    return pl.pallas_call(
        functools.partial(_ret_out_kernel, alpha=alpha),
        grid=(rows // tm,),
        in_specs=[
            pl.BlockSpec((tm, v_w), lambda t: (t, 0)),
            pl.BlockSpec((tm, d), lambda t: (t, 0)),
            _mod_spec(d, tiles_per_row),
            _const_spec((v_w, d)),
            _const_spec((1, d)),
            _const_spec((1, d)),
        ],
        out_specs=pl.BlockSpec((tm, d), lambda t: (t, 0)),
        out_shape=jax.ShapeDtypeStruct((rows, d), F32),
        compiler_params=_params(1),
        name="ret_out",
    )(y, x, mod, w_out, ln_g, ln_b)


def _rope_tables(seq, dk):
    t = jnp.arange(seq)
    row = (t // GRID_W).astype(F32)
    col = (t % GRID_W).astype(F32)
    n_freq = dk // 4
    inv = ROPE_BASE ** (-jnp.arange(n_freq, dtype=F32) / n_freq)
    ang = jnp.concatenate([row[:, None] * inv, col[:, None] * inv], -1)
    cos, sin = jnp.cos(ang), jnp.sin(ang)
    return cos, sin, cos.T, sin.T


def kernel(x, c, ctx, c_ctx, ada_w, ada_b, ln1_g, ln1_b, ln2_g, ln2_b, ffn_w1, ffn_w2,
           a_w_in, a_b_in, a_ln_g, a_ln_b, a_w_s, a_b_s, a_w_out, b_w_in, b_decay, b_w_out):
    n_batch, seq, d = x.shape
    ctx_len = ctx.shape[1]
    depth = ada_w.shape[0]
    alpha = float((2 * depth) ** 0.25)
    n_heads = b_decay.shape[-1]
    dk = d // n_heads
    qk_w = n_heads * dk
    d_a = a_w_out.shape[1]
    n_groups = a_w_s.shape[1]
    tm = ROW_TILE

    mod = _modulation(jnp.concatenate([c, c_ctx[None]], axis=0), ada_w, ada_b)
    mod = mod.reshape(depth, n_batch + 1, 6, d)
    rope_tabs = _rope_tables(seq, dk)

    xl = x.reshape(n_batch * seq, d)
    xc = ctx.reshape(n_batch * ctx_len, d)
    row = lambda p: p.reshape(1, -1)

    for i in range(depth):
        ctx_out = i < depth - 1
        mod_l, mod_c = mod[i, :n_batch], mod[i, n_batch:]
        j = i // 2
        g1, b1 = row(ln1_g[i]), row(ln1_b[i])
        if i % 2 == 0:
            bs_rows = jnp.repeat(a_b_s[j].T, d_a // n_groups, axis=1)
            weights = (a_w_in[j].astype(BF16), row(a_b_in[j]), row(a_ln_g[j]), row(a_ln_b[j]),
                       a_w_s[j].astype(BF16), bs_rows, a_w_out[j].astype(BF16), g1, b1)
            xl = _gmlp(xl, mod_l, *weights, alpha=alpha, tm=tm)
            if ctx_out:
                xc = _gmlp(xc, mod_c, *weights, alpha=alpha, tm=tm)
        else:
            w_in = b_w_in[j]
            wq = w_in[:, :qk_w].astype(BF16)
            wkt = w_in[:, qk_w:2 * qk_w].T.astype(BF16)
            wr = w_in[:, 2 * qk_w:].astype(BF16)
            w_out = b_w_out[j].astype(BF16)
            lat = _ret_proj(xl, mod_l, wq, wkt, wr, rope_tabs, n_batch=n_batch, n_heads=n_heads)
            cpt = _ret_proj(xc, mod_c, wq, wkt, wr, None, n_batch=n_batch, n_heads=n_heads)
            decay = b_decay[j].T.reshape(n_heads, 2, 1)
            y_l, y_c = _ret_scan(decay, cpt, lat, n_batch=n_batch, n_heads=n_heads, ctx_out=ctx_out)
            xl = _ret_out(y_l, xl, mod_l, w_out, g1, b1, alpha=alpha, tm=tm)
            if ctx_out:
                xc = _ret_out(y_c, xc, mod_c, w_out, g1, b1, alpha=alpha, tm=tm)
        ffn_args = (ffn_w1[i].astype(BF16), ffn_w2[i].astype(BF16), row(ln2_g[i]), row(ln2_b[i]))
        xl = _ffn(xl, mod_l, *ffn_args, alpha=alpha, tm=tm)
        if ctx_out:
            xc = _ffn(xc, mod_c, *ffn_args, alpha=alpha, tm=tm)
    return xl.reshape(n_batch, seq, d)
```

```python
import functools
import math

import jax
import jax.numpy as jnp
from jax import lax
from jax.experimental import pallas as pl
from jax.experimental.pallas import tpu as pltpu

F32 = jnp.float32
BF16 = jnp.bfloat16

LN_EPS = 1e-5
GRID_W = 64
ROPE_BASE = 10000.0
RET_CHUNK = 256
SCAN_UNROLL = 2
ROW_TILE = 256
GMLP_ROW_TILE = 512
GMLP_SUB_TILE = 256
FFN_COL_CHUNK = 1024
VMEM_LIMIT = 56 * 1024 * 1024


def _params(n_axes, vmem=VMEM_LIMIT):
    return pltpu.CompilerParams(dimension_semantics=("arbitrary",) * n_axes, vmem_limit_bytes=vmem)


def _const_spec(shape):
    zeros = (0,) * len(shape)
    return pl.BlockSpec(shape, lambda *_: zeros, pipeline_mode=pl.Buffered(1))


def _mod_spec(d, tiles_per_row):
    return pl.BlockSpec((None, 6, d), lambda t: (t // tiles_per_row, 0, 0))


def _layernorm(y, g, b):
    mu = jnp.mean(y, axis=-1, keepdims=True)
    yc = y - mu
    var = jnp.mean(yc * yc, axis=-1, keepdims=True)
    return yc * lax.rsqrt(var + LN_EPS) * g + b


def _modulate(x, shift, scale):
    return (x * (1.0 + scale) + shift).astype(BF16)


def _gelu(z):
    return 0.5 * z * (1.0 + lax.erf(z * math.sqrt(0.5)))


def _modulation_kernel(c_ref, w_ref, b_ref, o_ref):
    sc = c_ref[...]
    sc = sc * jax.nn.sigmoid(sc)
    o_ref[...] = jnp.dot(sc, w_ref[...], preferred_element_type=F32,
                         precision=lax.Precision.HIGHEST) + b_ref[...]


def _modulation(cc, ada_w, ada_b):
    depth, d, n = ada_w.shape
    nb = cc.shape[0]
    tn = 1536
    return pl.pallas_call(
        _modulation_kernel,
        grid=(depth, n // tn),
        in_specs=[
            pl.BlockSpec((nb, d), lambda i, j: (0, 0)),
            pl.BlockSpec((None, d, tn), lambda i, j: (i, 0, j)),
            pl.BlockSpec((None, 1, tn), lambda i, j: (i, 0, j)),
        ],
        out_specs=pl.BlockSpec((None, nb, tn), lambda i, j: (i, 0, j)),
        out_shape=jax.ShapeDtypeStruct((depth, nb, n), F32),
        compiler_params=_params(2),
        name="modulation",
    )(cc, ada_w, ada_b.reshape(depth, 1, n))


def _ffn_kernel(*refs, alpha, mixer_out):
    if mixer_out:
        y_ref, wo_ref, g1_ref, b1_ref, x_ref, mod_ref, w1_ref, w2_ref, g_ref, b_ref, o_ref = refs
        mix = jnp.dot(y_ref[...], wo_ref[...], preferred_element_type=F32)
        x = _layernorm(alpha * x_ref[...] + mod_ref[2:3, :] * mix, g1_ref[...], b1_ref[...])
    else:
        x_ref, mod_ref, w1_ref, w2_ref, g_ref, b_ref, o_ref = refs
        x = x_ref[...]
    h = _modulate(x, mod_ref[3:4, :], mod_ref[4:5, :])
    n_hidden = w1_ref.shape[1]
    acc = None
    for c0 in range(0, n_hidden, FFN_COL_CHUNK):
        z = jnp.dot(h, w1_ref[:, c0:c0 + FFN_COL_CHUNK], preferred_element_type=F32)
        a = jnp.maximum(z, 0.0)
        a = (a * a).astype(BF16)
        part = jnp.dot(a, w2_ref[c0:c0 + FFN_COL_CHUNK, :], preferred_element_type=F32)
        acc = part if acc is None else acc + part
    y = alpha * x + mod_ref[5:6, :] * acc
    o_ref[...] = _layernorm(y, g_ref[...], b_ref[...])


def _ffn(x, mod, w1, w2, ln_g, ln_b, *, alpha, tm, mixer=None):
    rows, d = x.shape
    f = w1.shape[1]
    tiles_per_row = rows // mod.shape[0] // tm
    in_specs = [
        pl.BlockSpec((tm, d), lambda t: (t, 0)),
        _mod_spec(d, tiles_per_row),
        _const_spec((d, f)),
        _const_spec((f, d)),
        _const_spec((1, d)),
        _const_spec((1, d)),
    ]
    args = [x, mod, w1, w2, ln_g, ln_b]
    if mixer is not None:
        v_w = mixer[0].shape[1]
        in_specs = [pl.BlockSpec((tm, v_w), lambda t: (t, 0)), _const_spec((v_w, d)),
                    _const_spec((1, d)), _const_spec((1, d))] + in_specs
        args = list(mixer) + args
    return pl.pallas_call(
        functools.partial(_ffn_kernel, alpha=alpha, mixer_out=mixer is not None),
        grid=(rows // tm,),
        in_specs=in_specs,
        out_specs=pl.BlockSpec((tm, d), lambda t: (t, 0)),
        out_shape=jax.ShapeDtypeStruct((rows, d), F32),
        compiler_params=_params(1),
        name="ffn" if mixer is None else "mix_ffn",
    )(*args)


def _gmlp_kernel(x_ref, mod_ref, win_ref, bin_ref, lng_ref, lnb_ref, ws_ref, bs_ref, wout_ref,
                 g_ref, b_ref, o_ref, s_ref, *, alpha):
    d_a = wout_ref.shape[0]
    n_groups, chunk, _ = ws_ref.shape
    cg = d_a // n_groups
    ck = 1024
    for t0 in range(0, x_ref.shape[0], GMLP_SUB_TILE):
        rows = slice(t0, t0 + GMLP_SUB_TILE)
        x = x_ref[rows, :]
        h = _modulate(x, mod_ref[0:1, :], mod_ref[1:2, :])
        v = jnp.dot(h, win_ref[:, d_a:], preferred_element_type=F32) + bin_ref[:, d_a:]
        v = _layernorm(_gelu(v), lng_ref[...], lnb_ref[...]).astype(BF16)
        for r0 in range(0, GMLP_SUB_TILE, chunk):
            for g in range(n_groups):
                s_ref[t0 + r0:t0 + r0 + chunk, g * cg:(g + 1) * cg] = (
                    jnp.dot(ws_ref[g], v[r0:r0 + chunk, g * cg:(g + 1) * cg], preferred_element_type=F32)
                    + bs_ref[:, g * cg:(g + 1) * cg])
        acc = None
        for c0 in range(0, d_a, ck):
            u = _gelu(jnp.dot(h, win_ref[:, c0:c0 + ck], preferred_element_type=F32) + bin_ref[:, c0:c0 + ck])
            gated = (u * s_ref[rows, c0:c0 + ck]).astype(BF16)
            part = jnp.dot(gated, wout_ref[c0:c0 + ck, :], preferred_element_type=F32)
            acc = part if acc is None else acc + part
        y = alpha * x + mod_ref[2:3, :] * acc
        o_ref[rows, :] = _layernorm(y, g_ref[...], b_ref[...])


def _gmlp(x, mod, w_in, b_in, ln_g, ln_b, w_s, b_s_rows, w_out, ln1_g, ln1_b, *, alpha, tm):
    rows, d = x.shape
    d_a = w_out.shape[0]
    n_groups, chunk, _ = w_s.shape
    tiles_per_row = rows // mod.shape[0] // tm
    return pl.pallas_call(
        functools.partial(_gmlp_kernel, alpha=alpha),
        grid=(rows // tm,),
        in_specs=[
            pl.BlockSpec((tm, d), lambda t: (t, 0)),
            _mod_spec(d, tiles_per_row),
            _const_spec((d, 2 * d_a)),
            _const_spec((1, 2 * d_a)),
            _const_spec((1, d_a)),
            _const_spec((1, d_a)),
            _const_spec((n_groups, chunk, chunk)),
            _const_spec((chunk, d_a)),
            _const_spec((d_a, d)),
            _const_spec((1, d)),
            _const_spec((1, d)),
        ],
        out_specs=pl.BlockSpec((tm, d), lambda t: (t, 0)),
        out_shape=jax.ShapeDtypeStruct((rows, d), F32),
        scratch_shapes=[pltpu.VMEM((tm, d_a), F32)],
        compiler_params=_params(1),
        name="gmlp",
    )(x, mod, w_in, b_in, ln_g, ln_b, w_s, b_s_rows, w_out, ln1_g, ln1_b)


def _rope_pairs(t1, t2, cos, sin):
    return t1 * cos - t2 * sin, t1 * sin + t2 * cos


def _ret_proj_kernel(*refs, rope, n_heads):
    if rope:
        (x_ref, mod_ref, wq_ref, wkt_ref, wr_ref, cos_ref, sin_ref, cost_ref, sint_ref,
         q_ref, kt_ref, v_ref, gf_ref, gb_ref) = refs
    else:
        x_ref, mod_ref, wq_ref, wkt_ref, wr_ref, q_ref, kt_ref, v_ref, gf_ref, gb_ref = refs
    qk_w = wq_ref.shape[1]
    dk = qk_w // n_heads
    half = dk // 2
    v_w = v_ref.shape[1]
    h = _modulate(x_ref[...], mod_ref[0:1, :], mod_ref[1:2, :])

    q = jnp.dot(h, wq_ref[...], preferred_element_type=F32)
    kt = lax.dot_general(wkt_ref[...], h, (((1,), (1,)), ((), ())), preferred_element_type=F32)
    kt = kt * (dk ** -0.5)
    if rope:
        cos, sin = cos_ref[...], sin_ref[...]
        cost, sint = cost_ref[...], sint_ref[...]
        for hh in range(n_heads):
            a, b = hh * dk, hh * dk + half
            o1, o2 = _rope_pairs(q[:, a:b], q[:, b:b + half], cos, sin)
            q_ref[:, a:b] = o1.astype(BF16)
            q_ref[:, b:b + half] = o2.astype(BF16)
            o1, o2 = _rope_pairs(kt[a:b, :], kt[b:b + half, :], cost, sint)
            kt_ref[a:b, :] = o1.astype(BF16)
            kt_ref[b:b + half, :] = o2.astype(BF16)
    else:
        q_ref[...] = q.astype(BF16)
        kt_ref[...] = kt.astype(BF16)
    v_ref[...] = jnp.dot(h, wr_ref[:, :v_w], preferred_element_type=F32).astype(BF16)
    for j, ref in enumerate((gf_ref, gb_ref), start=1):
        gate = jnp.dot(h, wr_ref[:, j * v_w:(j + 1) * v_w], preferred_element_type=F32)
        ref[...] = (gate * jax.nn.sigmoid(gate)).astype(BF16)


def _ret_proj(x, mod, wq, wkt, wr, rope_tabs, *, n_batch, n_heads):
    rows, d = x.shape
    tm = RET_CHUNK
    seq = rows // n_batch
    npb = seq // tm
    qk_w = wq.shape[1]
    v_w = wr.shape[1] // 3
    tiles_per_row = rows // mod.shape[0] // tm
    in_specs = [
        pl.BlockSpec((tm, d), lambda t: (t, 0)),
        _mod_spec(d, tiles_per_row),
        _const_spec((d, qk_w)),
        _const_spec((qk_w, d)),
        _const_spec((d, 3 * v_w)),
    ]
    args = [x, mod, wq, wkt, wr]
    rope = rope_tabs is not None
    if rope:
        half = qk_w // n_heads // 2
        in_specs += [
            pl.BlockSpec((tm, half), lambda t: (t % npb, 0)),
            pl.BlockSpec((tm, half), lambda t: (t % npb, 0)),
            pl.BlockSpec((half, tm), lambda t: (0, t % npb)),
            pl.BlockSpec((half, tm), lambda t: (0, t % npb)),
        ]
        args += list(rope_tabs)
    row_spec = lambda w: pl.BlockSpec((tm, w), lambda t: (t, 0))
    return pl.pallas_call(
        functools.partial(_ret_proj_kernel, rope=rope, n_heads=n_heads),
        grid=(rows // tm,),
        in_specs=in_specs,
        out_specs=[
            row_spec(qk_w),
            pl.BlockSpec((None, None, qk_w, tm), lambda t: (t // npb, t % npb, 0, 0)),
            row_spec(v_w), row_spec(v_w), row_spec(v_w),
        ],
        out_shape=[
            jax.ShapeDtypeStruct((rows, qk_w), BF16),
            jax.ShapeDtypeStruct((n_batch, npb, qk_w, tm), BF16),
            jax.ShapeDtypeStruct((rows, v_w), BF16),
            jax.ShapeDtypeStruct((rows, v_w), BF16),
            jax.ShapeDtypeStruct((rows, v_w), BF16),
        ],
        compiler_params=_params(1),
        name="ret_proj_rope" if rope else "ret_proj",
    )(*args)


def _ret_scan_kernel(*refs, ctx_out):
    (dec_ref, qc_ref, kc_ref, vc_ref, gfc_ref, gbc_ref, ql_ref, kl_ref, vl_ref, gfl_ref, gbl_ref) = refs[:11]
    if ctx_out:
        yl_ref, yc_ref = refs[11:13]
        scratch = refs[13:]
    else:
        yl_ref, yc_ref = refs[11], None
        scratch = refs[12:]
    mask_ref, xi_ref, zeta_ref, s_ref, yacc_ref = scratch
    n_c, dk, c = kc_ref.shape
    n_l = kl_ref.shape[0]
    lc = n_c * c

    def log_gamma():
        return jnp.log1p(-jnp.exp(dec_ref[...]))

    @pl.when(pl.program_id(1) == 0)
    def _():
        lg = log_gamma()
        lgf, lgb = lg[0:1, :], lg[1:2, :]
        ii = lax.broadcasted_iota(jnp.int32, (c, c), 0).astype(F32)
        jj = lax.broadcasted_iota(jnp.int32, (c, c), 1).astype(F32)
        diff = ii - jj
        mask_ref[0] = jnp.where(diff >= 0, jnp.exp(lgf * jnp.maximum(diff, 0.0)), 0.0)
        mask_ref[1] = jnp.where(diff <= 0, jnp.exp(lgb * jnp.maximum(-diff, 0.0)), 0.0)
        pi = lax.broadcasted_iota(jnp.int32, (c, dk), 0).astype(F32)
        xi_ref[0] = jnp.exp(lgf * (pi + 1.0))
        xi_ref[1] = jnp.exp(lgb * (c - pi))
        pj = lax.broadcasted_iota(jnp.int32, (8, c), 1).astype(F32)
        zeta_ref[0] = jnp.exp(lgf * (c - 1.0 - pj))
        zeta_ref[1] = jnp.exp(lgb * pj)

    gamma_c = jnp.exp(log_gamma() * float(c))

    def chunk_step(q_ref, k_ref, v_ref, g_ref, y_ref, ci, d, row_off, emit):
        r0 = ci * c if isinstance(ci, int) else pl.multiple_of(ci * c, c)
        qc = q_ref[pl.ds(r0, c), :]
        kt = k_ref[ci]
        vc = v_ref[pl.ds(r0, c), :]
        state = s_ref[d]
        kz = (kt.astype(F32) * zeta_ref[d, 0:1, :]).astype(BF16)
        s_ref[d] = state * gamma_c[d:d + 1, :] + jnp.dot(kz, vc, preferred_element_type=F32)
        if emit is None:
            return
        scores = jnp.dot(qc, kt, preferred_element_type=F32)
        p = (scores * mask_ref[d]).astype(BF16)
        qx = (qc.astype(F32) * xi_ref[d]).astype(BF16)
        o = (jnp.dot(p, vc, preferred_element_type=F32)
             + jnp.dot(qx, state.astype(BF16), preferred_element_type=F32))
        mu = jnp.mean(o, axis=-1, keepdims=True)
        oc = o - mu
        var = jnp.mean(oc * oc, axis=-1, keepdims=True)
        gated = oc * lax.rsqrt(var + LN_EPS) * g_ref[pl.ds(r0, c), :].astype(F32)
        parked = pl.ds(row_off + r0, c)
        if emit == "first":
            yacc_ref[parked, :] = gated
        else:
            y_ref[pl.ds(r0, c), :] = (yacc_ref[parked, :] + gated).astype(BF16)

    s_ref[...] = jnp.zeros_like(s_ref)
    seen = set()
    for ci in range(n_c):
        for d, g_ref, cj in ((0, gfc_ref, ci), (1, gbc_ref, n_c - 1 - ci)):
            emit = None if not ctx_out else ("second" if cj in seen else "first")
            seen.add(cj)
            chunk_step(qc_ref, kc_ref, vc_ref, g_ref, yc_ref, cj, d, 0, emit)

    def pair(emit):
        def body(i, carry):
            chunk_step(ql_ref, kl_ref, vl_ref, gfl_ref, yl_ref, i, 0, lc, emit)
            chunk_step(ql_ref, kl_ref, vl_ref, gbl_ref, yl_ref, n_l - 1 - i, 1, lc, emit)
            return carry
        return body

    lax.fori_loop(0, n_l // 2, pair("first"), 0, unroll=SCAN_UNROLL)
    lax.fori_loop(n_l // 2, n_l, pair("second"), 0, unroll=SCAN_UNROLL)


def _ret_scan(decay, ctx_parts, lat_parts, *, n_batch, n_heads, ctx_out):
    qc, kc, vc, gfc, gbc = ctx_parts
    ql, kl, vl, gfl, gbl = lat_parts
    _, n_c, qk_w, c = kc.shape
    n_l = kl.shape[1]
    assert n_l % 2 == 0, "the latent walk is split into a first and a second half"
    dk = qk_w // n_heads
    dv = vl.shape[1] // n_heads
    lc, ll = n_c * c, n_l * c

    def seq_specs(seq, n_chunks):
        row = lambda w: pl.BlockSpec((seq, w), lambda h, b: (b, h))
        return [row(dk), pl.BlockSpec((None, n_chunks, dk, c), lambda h, b: (b, 0, h, 0)),
                row(dv), row(dv), row(dv)]

    out_specs = [pl.BlockSpec((ll, dv), lambda h, b: (b, h))]
    out_shape = [jax.ShapeDtypeStruct((n_batch * ll, n_heads * dv), BF16)]
    if ctx_out:
        out_specs.append(pl.BlockSpec((lc, dv), lambda h, b: (b, h)))
        out_shape.append(jax.ShapeDtypeStruct((n_batch * lc, n_heads * dv), BF16))
    res = pl.pallas_call(
        functools.partial(_ret_scan_kernel, ctx_out=ctx_out),
        grid=(n_heads, n_batch),
        in_specs=[pl.BlockSpec((None, 2, 1), lambda h, b: (h, 0, 0))] + seq_specs(lc, n_c) + seq_specs(ll, n_l),
        out_specs=out_specs,
        out_shape=out_shape,
        scratch_shapes=[
            pltpu.VMEM((2, c, c), F32),
            pltpu.VMEM((2, c, dk), F32),
            pltpu.VMEM((2, 8, c), F32),
            pltpu.VMEM((2, dk, dv), F32),
            pltpu.VMEM((lc + ll, dv), F32),
        ],
        compiler_params=_params(2),
        name="ret_scan",
    )(decay, qc, kc, vc, gfc, gbc, ql, kl, vl, gfl, gbl)
    return res if ctx_out else (res[0], None)


def _rope_tables(seq, dk):
    t = jnp.arange(seq)
    row = (t // GRID_W).astype(F32)
    col = (t % GRID_W).astype(F32)
    n_freq = dk // 4
    inv = ROPE_BASE ** (-jnp.arange(n_freq, dtype=F32) / n_freq)
    ang = jnp.concatenate([row[:, None] * inv, col[:, None] * inv], -1)
    cos, sin = jnp.cos(ang), jnp.sin(ang)
    return cos, sin, cos.T, sin.T


def kernel(x, c, ctx, c_ctx, ada_w, ada_b, ln1_g, ln1_b, ln2_g, ln2_b, ffn_w1, ffn_w2,
           a_w_in, a_b_in, a_ln_g, a_ln_b, a_w_s, a_b_s, a_w_out, b_w_in, b_decay, b_w_out):
    n_batch, seq, d = x.shape
    ctx_len = ctx.shape[1]
    depth = ada_w.shape[0]
    alpha = float((2 * depth) ** 0.25)
    n_heads = b_decay.shape[-1]
    dk = d // n_heads
    qk_w = n_heads * dk
    d_a = a_w_out.shape[1]
    n_groups = a_w_s.shape[1]
    tm = ROW_TILE

    mod = _modulation(jnp.concatenate([c, c_ctx[None]], axis=0), ada_w, ada_b)
    mod = mod.reshape(depth, n_batch + 1, 6, d)
    rope_tabs = _rope_tables(seq, dk)

    xl = x.reshape(n_batch * seq, d)
    xc = ctx.reshape(n_batch * ctx_len, d)
    row = lambda p: p.reshape(1, -1)

    for i in range(depth):
        ctx_out = i < depth - 1
        mod_l, mod_c = mod[i, :n_batch], mod[i, n_batch:]
        j = i // 2
        g1, b1 = row(ln1_g[i]), row(ln1_b[i])
        if i % 2 == 0:
            bs_rows = jnp.repeat(a_b_s[j].T, d_a // n_groups, axis=1)
            weights = (a_w_in[j].astype(BF16), row(a_b_in[j]), row(a_ln_g[j]), row(a_ln_b[j]),
                       a_w_s[j].astype(BF16), bs_rows, a_w_out[j].astype(BF16), g1, b1)
            xl = _gmlp(xl, mod_l, *weights, alpha=alpha, tm=GMLP_ROW_TILE)
            if ctx_out:
                xc = _gmlp(xc, mod_c, *weights, alpha=alpha, tm=GMLP_ROW_TILE)
            mix_l = mix_c = None
        else:
            w_in = b_w_in[j]
            wq = w_in[:, :qk_w].astype(BF16)
            wkt = w_in[:, qk_w:2 * qk_w].T.astype(BF16)
            wr = w_in[:, 2 * qk_w:].astype(BF16)
            w_out = b_w_out[j].astype(BF16)
            lat = _ret_proj(xl, mod_l, wq, wkt, wr, rope_tabs, n_batch=n_batch, n_heads=n_heads)
            cpt = _ret_proj(xc, mod_c, wq, wkt, wr, None, n_batch=n_batch, n_heads=n_heads)
            decay = b_decay[j].T.reshape(n_heads, 2, 1)
            y_l, y_c = _ret_scan(decay, cpt, lat, n_batch=n_batch, n_heads=n_heads, ctx_out=ctx_out)
            mix_l, mix_c = (y_l, w_out, g1, b1), (y_c, w_out, g1, b1)
        ffn_args = (ffn_w1[i].astype(BF16), ffn_w2[i].astype(BF16), row(ln2_g[i]), row(ln2_b[i]))
        xl = _ffn(xl, mod_l, *ffn_args, alpha=alpha, tm=tm, mixer=mix_l)
        if ctx_out:
            xc = _ffn(xc, mod_c, *ffn_args, alpha=alpha, tm=tm, mixer=mix_c)
    return xl.reshape(n_batch, seq, d)
```

```python
import functools
import math

import jax
import jax.numpy as jnp
from jax import lax
from jax.experimental import pallas as pl
from jax.experimental.pallas import tpu as pltpu

F32 = jnp.float32
BF16 = jnp.bfloat16

LN_EPS = 1e-5
GRID_W = 64
ROPE_BASE = 10000.0
RET_CHUNK = 256
SCAN_UNROLL = 4
SUB_TILE = 256
FFN_ROW_TILE = 1024
GMLP_ROW_TILE = 512
GMLP_COL_CHUNK = 1024
FFN_COL_CHUNK = 1024
VMEM_LIMIT = 56 * 1024 * 1024


def _params(n_axes, vmem=VMEM_LIMIT):
    return pltpu.CompilerParams(dimension_semantics=("arbitrary",) * n_axes, vmem_limit_bytes=vmem)


def _const_spec(shape):
    zeros = (0,) * len(shape)
    return pl.BlockSpec(shape, lambda *_: zeros, pipeline_mode=pl.Buffered(1))


def _mod_spec(d, tiles_per_row):
    return pl.BlockSpec((None, 6, d), lambda t: (t // tiles_per_row, 0, 0))


def _row_tiling(rows, n_mod_rows, tm):
    rows_per_mod = rows // n_mod_rows
    tm = min(tm, rows_per_mod)
    assert rows_per_mod % tm == 0 and tm % SUB_TILE == 0, (rows, n_mod_rows, tm)
    return tm, rows_per_mod // tm


def _layernorm(y, g, b):
    mu = jnp.mean(y, axis=-1, keepdims=True)
    yc = y - mu
    var = jnp.mean(yc * yc, axis=-1, keepdims=True)
    return yc * lax.rsqrt(var + LN_EPS) * g + b


def _modulate(x, shift, scale):
    return (x * (1.0 + scale) + shift).astype(BF16)


def _gelu(z):
    return 0.5 * z * (1.0 + lax.erf(z * math.sqrt(0.5)))


def _modulation_kernel(c_ref, w_ref, b_ref, o_ref):
    sc = c_ref[...]
    sc = sc * jax.nn.sigmoid(sc)
    o_ref[...] = jnp.dot(sc, w_ref[...], preferred_element_type=F32,
                         precision=lax.Precision.HIGHEST) + b_ref[...]


def _modulation(cc, ada_w, ada_b):
    depth, d, n = ada_w.shape
    nb = cc.shape[0]
    tn = 1536
    return pl.pallas_call(
        _modulation_kernel,
        grid=(depth, n // tn),
        in_specs=[
            pl.BlockSpec((nb, d), lambda i, j: (0, 0)),
            pl.BlockSpec((None, d, tn), lambda i, j: (i, 0, j)),
            pl.BlockSpec((None, 1, tn), lambda i, j: (i, 0, j)),
        ],
        out_specs=pl.BlockSpec((None, nb, tn), lambda i, j: (i, 0, j)),
        out_shape=jax.ShapeDtypeStruct((depth, nb, n), F32),
        compiler_params=_params(2),
        name="modulation",
    )(cc, ada_w, ada_b.reshape(depth, 1, n))


def _ffn_kernel(*refs, alpha, mixer_out):
    if mixer_out:
        y_ref, wo_ref, g1_ref, b1_ref, x_ref, mod_ref, w1_ref, w2_ref, g_ref, b_ref, o_ref = refs
    else:
        x_ref, mod_ref, w1_ref, w2_ref, g_ref, b_ref, o_ref = refs
    n_hidden = w1_ref.shape[1]
    n_sub = x_ref.shape[0] // SUB_TILE

    def prologue(s):
        rows = slice(s * SUB_TILE, (s + 1) * SUB_TILE)
        x = x_ref[rows, :]
        if mixer_out:
            mix = jnp.dot(y_ref[rows, :], wo_ref[...], preferred_element_type=F32)
            x = _layernorm(alpha * x + mod_ref[2:3, :] * mix, g1_ref[...], b1_ref[...])
        return x, _modulate(x, mod_ref[3:4, :], mod_ref[4:5, :])

    def epilogue(s, x, acc):
        y = alpha * x + mod_ref[5:6, :] * acc
        o_ref[s * SUB_TILE:(s + 1) * SUB_TILE, :] = _layernorm(y, g_ref[...], b_ref[...])

    chunks = list(range(0, n_hidden, FFN_COL_CHUNK))
    nxt = prologue(0)
    done = None
    for s in range(n_sub):
        x, h = nxt
        acc = None
        for k, c0 in enumerate(chunks):
            z = jnp.dot(h, w1_ref[:, c0:c0 + FFN_COL_CHUNK], preferred_element_type=F32)
            a = jnp.maximum(z, 0.0)
            a = (a * a).astype(BF16)
            part = jnp.dot(a, w2_ref[c0:c0 + FFN_COL_CHUNK, :], preferred_element_type=F32)
            acc = part if acc is None else acc + part
            if k == 0 and done is not None:
                epilogue(*done)
            if k == len(chunks) // 2 and s + 1 < n_sub:
                nxt = prologue(s + 1)
        done = (s, x, acc)
    epilogue(*done)


def _ffn(x, mod, w1, w2, ln_g, ln_b, *, alpha, tm, mixer=None):
    rows, d = x.shape
    f = w1.shape[1]
    tm, tiles_per_row = _row_tiling(rows, mod.shape[0], tm)
    in_specs = [
        pl.BlockSpec((tm, d), lambda t: (t, 0)),
        _mod_spec(d, tiles_per_row),
        _const_spec((d, f)),
        _const_spec((f, d)),
        _const_spec((1, d)),
        _const_spec((1, d)),
    ]
    args = [x, mod, w1, w2, ln_g, ln_b]
    if mixer is not None:
        v_w = mixer[0].shape[1]
        in_specs = [pl.BlockSpec((tm, v_w), lambda t: (t, 0)), _const_spec((v_w, d)),
                    _const_spec((1, d)), _const_spec((1, d))] + in_specs
        args = list(mixer) + args
    return pl.pallas_call(
        functools.partial(_ffn_kernel, alpha=alpha, mixer_out=mixer is not None),
        grid=(rows // tm,),
        in_specs=in_specs,
        out_specs=pl.BlockSpec((tm, d), lambda t: (t, 0)),
        out_shape=jax.ShapeDtypeStruct((rows, d), F32),
        compiler_params=_params(1),
        name="ffn" if mixer is None else "mix_ffn",
    )(*args)


def _gmlp_kernel(x_ref, mod_ref, win_ref, bin_ref, lng_ref, lnb_ref, ws_ref, bs_ref, wout_ref,
                 g_ref, b_ref, o_ref, s_ref, *, alpha):
    d_a = wout_ref.shape[0]
    n_groups, chunk, _ = ws_ref.shape
    cg = d_a // n_groups
    n_sub = x_ref.shape[0] // SUB_TILE

    def gate_stage(s):
        t0 = s * SUB_TILE
        x = x_ref[t0:t0 + SUB_TILE, :]
        h = _modulate(x, mod_ref[0:1, :], mod_ref[1:2, :])
        v = jnp.dot(h, win_ref[:, d_a:], preferred_element_type=F32) + bin_ref[:, d_a:]
        v = _layernorm(_gelu(v), lng_ref[...], lnb_ref[...]).astype(BF16)
        for r0 in range(0, SUB_TILE, chunk):
            for g in range(n_groups):
                s_ref[t0 + r0:t0 + r0 + chunk, g * cg:(g + 1) * cg] = (
                    jnp.dot(ws_ref[g], v[r0:r0 + chunk, g * cg:(g + 1) * cg], preferred_element_type=F32)
                    + bs_ref[:, g * cg:(g + 1) * cg])
        return x, h

    def epilogue(s, x, acc):
        y = alpha * x + mod_ref[2:3, :] * acc
        o_ref[s * SUB_TILE:(s + 1) * SUB_TILE, :] = _layernorm(y, g_ref[...], b_ref[...])

    for s in range(n_sub):
        x, h = gate_stage(s)
        rows = slice(s * SUB_TILE, (s + 1) * SUB_TILE)
        acc = None
        for c0 in range(0, d_a, GMLP_COL_CHUNK):
            cols = slice(c0, c0 + GMLP_COL_CHUNK)
            u = _gelu(jnp.dot(h, win_ref[:, cols], preferred_element_type=F32) + bin_ref[:, cols])
            gated = (u * s_ref[rows, cols]).astype(BF16)
            part = jnp.dot(gated, wout_ref[cols, :], preferred_element_type=F32)
            acc = part if acc is None else acc + part
        epilogue(s, x, acc)


def _gmlp(x, mod, w_in, b_in, ln_g, ln_b, w_s, b_s_rows, w_out, ln1_g, ln1_b, *, alpha, tm):
    rows, d = x.shape
    d_a = w_out.shape[0]
    n_groups, chunk, _ = w_s.shape
    tm, tiles_per_row = _row_tiling(rows, mod.shape[0], tm)
    return pl.pallas_call(
        functools.partial(_gmlp_kernel, alpha=alpha),
        grid=(rows // tm,),
        in_specs=[
            pl.BlockSpec((tm, d), lambda t: (t, 0)),
            _mod_spec(d, tiles_per_row),
            _const_spec((d, 2 * d_a)),
            _const_spec((1, 2 * d_a)),
            _const_spec((1, d_a)),
            _const_spec((1, d_a)),
            _const_spec((n_groups, chunk, chunk)),
            _const_spec((chunk, d_a)),
            _const_spec((d_a, d)),
            _const_spec((1, d)),
            _const_spec((1, d)),
        ],
        out_specs=pl.BlockSpec((tm, d), lambda t: (t, 0)),
        out_shape=jax.ShapeDtypeStruct((rows, d), F32),
        scratch_shapes=[pltpu.VMEM((tm, d_a), F32)],
        compiler_params=_params(1),
        name="gmlp",
    )(x, mod, w_in, b_in, ln_g, ln_b, w_s, b_s_rows, w_out, ln1_g, ln1_b)


def _rope_pairs(t1, t2, cos, sin):
    return t1 * cos - t2 * sin, t1 * sin + t2 * cos


def _ret_proj_kernel(*refs, rope, n_heads):
    if rope:
        (x_ref, mod_ref, wq_ref, wkt_ref, wr_ref, cos_ref, sin_ref, cost_ref, sint_ref,
         q_ref, kt_ref, v_ref, gf_ref, gb_ref) = refs
    else:
        x_ref, mod_ref, wq_ref, wkt_ref, wr_ref, q_ref, kt_ref, v_ref, gf_ref, gb_ref = refs
    qk_w = wq_ref.shape[1]
    dk = qk_w // n_heads
    half = dk // 2
    v_w = v_ref.shape[1]
    h = _modulate(x_ref[...], mod_ref[0:1, :], mod_ref[1:2, :])

    q = jnp.dot(h, wq_ref[...], preferred_element_type=F32)
    kt = lax.dot_general(wkt_ref[...], h, (((1,), (1,)), ((), ())), preferred_element_type=F32)
    kt = kt * (dk ** -0.5)
    if rope:
        cos, sin = cos_ref[...], sin_ref[...]
        cost, sint = cost_ref[...], sint_ref[...]
        for hh in range(n_heads):
            a, b = hh * dk, hh * dk + half
            o1, o2 = _rope_pairs(q[:, a:b], q[:, b:b + half], cos, sin)
            q_ref[:, a:b] = o1.astype(BF16)
            q_ref[:, b:b + half] = o2.astype(BF16)
            o1, o2 = _rope_pairs(kt[a:b, :], kt[b:b + half, :], cost, sint)
            kt_ref[a:b, :] = o1.astype(BF16)
            kt_ref[b:b + half, :] = o2.astype(BF16)
    else:
        q_ref[...] = q.astype(BF16)
        kt_ref[...] = kt.astype(BF16)
    v_ref[...] = jnp.dot(h, wr_ref[:, :v_w], preferred_element_type=F32).astype(BF16)
    for j, ref in enumerate((gf_ref, gb_ref), start=1):
        gate = jnp.dot(h, wr_ref[:, j * v_w:(j + 1) * v_w], preferred_element_type=F32)
        ref[...] = (gate * jax.nn.sigmoid(gate)).astype(BF16)


def _ret_proj(x, mod, wq, wkt, wr, rope_tabs, *, n_batch, n_heads):
    rows, d = x.shape
    tm = RET_CHUNK
    seq = rows // n_batch
    npb = seq // tm
    qk_w = wq.shape[1]
    v_w = wr.shape[1] // 3
    tiles_per_row = rows // mod.shape[0] // tm
    in_specs = [
        pl.BlockSpec((tm, d), lambda t: (t, 0)),
        _mod_spec(d, tiles_per_row),
        _const_spec((d, qk_w)),
        _const_spec((qk_w, d)),
        _const_spec((d, 3 * v_w)),
    ]
    args = [x, mod, wq, wkt, wr]
    rope = rope_tabs is not None
    if rope:
        half = qk_w // n_heads // 2
        in_specs += [
            pl.BlockSpec((tm, half), lambda t: (t % npb, 0)),
            pl.BlockSpec((tm, half), lambda t: (t % npb, 0)),
            pl.BlockSpec((half, tm), lambda t: (0, t % npb)),
            pl.BlockSpec((half, tm), lambda t: (0, t % npb)),
        ]
        args += list(rope_tabs)
    row_spec = lambda w: pl.BlockSpec((tm, w), lambda t: (t, 0))
    return pl.pallas_call(
        functools.partial(_ret_proj_kernel, rope=rope, n_heads=n_heads),
        grid=(rows // tm,),
        in_specs=in_specs,
        out_specs=[
            row_spec(qk_w),
            pl.BlockSpec((None, None, qk_w, tm), lambda t: (t // npb, t % npb, 0, 0)),
            row_spec(v_w), row_spec(v_w), row_spec(v_w),
        ],
        out_shape=[
            jax.ShapeDtypeStruct((rows, qk_w), BF16),
            jax.ShapeDtypeStruct((n_batch, npb, qk_w, tm), BF16),
            jax.ShapeDtypeStruct((rows, v_w), BF16),
            jax.ShapeDtypeStruct((rows, v_w), BF16),
            jax.ShapeDtypeStruct((rows, v_w), BF16),
        ],
        compiler_params=_params(1),
        name="ret_proj_rope" if rope else "ret_proj",
    )(*args)


def _ret_scan_kernel(*refs, ctx_out):
    (dec_ref, qc_ref, kc_ref, vc_ref, gfc_ref, gbc_ref, ql_ref, kl_ref, vl_ref, gfl_ref, gbl_ref) = refs[:11]
    if ctx_out:
        yl_ref, yc_ref = refs[11:13]
        scratch = refs[13:]
    else:
        yl_ref, yc_ref = refs[11], None
        scratch = refs[12:]
    mask_ref, xi_ref, zeta_ref, s_ref, yacc_ref = scratch
    n_c, dk, c = kc_ref.shape
    n_l = kl_ref.shape[0]
    lc = n_c * c

    def log_gamma():
        return jnp.log1p(-jnp.exp(dec_ref[...]))

    @pl.when(pl.program_id(1) == 0)
    def _():
        lg = log_gamma()
        lgf, lgb = lg[0:1, :], lg[1:2, :]
        ii = lax.broadcasted_iota(jnp.int32, (c, c), 0).astype(F32)
        jj = lax.broadcasted_iota(jnp.int32, (c, c), 1).astype(F32)
        diff = ii - jj
        mask_ref[0] = jnp.where(diff >= 0, jnp.exp(lgf * jnp.maximum(diff, 0.0)), 0.0)
        mask_ref[1] = jnp.where(diff <= 0, jnp.exp(lgb * jnp.maximum(-diff, 0.0)), 0.0)
        pi = lax.broadcasted_iota(jnp.int32, (c, dk), 0).astype(F32)
        xi_ref[0] = jnp.exp(lgf * (pi + 1.0))
        xi_ref[1] = jnp.exp(lgb * (c - pi))
        pj = lax.broadcasted_iota(jnp.int32, (8, c), 1).astype(F32)
        zeta_ref[0] = jnp.exp(lgf * (c - 1.0 - pj))
        zeta_ref[1] = jnp.exp(lgb * pj)

    gamma_c = jnp.exp(log_gamma() * float(c))

    def chunk_step(q_ref, k_ref, v_ref, g_ref, y_ref, ci, d, row_off, emit):
        r0 = ci * c if isinstance(ci, int) else pl.multiple_of(ci * c, c)
        qc = q_ref[pl.ds(r0, c), :]
        kt = k_ref[ci]
        vc = v_ref[pl.ds(r0, c), :]
        state = s_ref[d]
        kz = (kt.astype(F32) * zeta_ref[d, 0:1, :]).astype(BF16)
        s_ref[d] = state * gamma_c[d:d + 1, :] + jnp.dot(kz, vc, preferred_element_type=F32)
        if emit is None:
            return
        scores = jnp.dot(qc, kt, preferred_element_type=F32)
        p = (scores * mask_ref[d]).astype(BF16)
        qx = (qc.astype(F32) * xi_ref[d]).astype(BF16)
        o = (jnp.dot(p, vc, preferred_element_type=F32)
             + jnp.dot(qx, state.astype(BF16), preferred_element_type=F32))
        mu = jnp.mean(o, axis=-1, keepdims=True)
        oc = o - mu
        var = jnp.mean(oc * oc, axis=-1, keepdims=True)
        gated = oc * lax.rsqrt(var + LN_EPS) * g_ref[pl.ds(r0, c), :].astype(F32)
        parked = pl.ds(row_off + r0, c)
        if emit == "first":
            yacc_ref[parked, :] = gated
        else:
            y_ref[pl.ds(r0, c), :] = (yacc_ref[parked, :] + gated).astype(BF16)

    s_ref[...] = jnp.zeros_like(s_ref)
    seen = set()
    for ci in range(n_c):
        for d, g_ref, cj in ((0, gfc_ref, ci), (1, gbc_ref, n_c - 1 - ci)):
            emit = None if not ctx_out else ("second" if cj in seen else "first")
            seen.add(cj)
            chunk_step(qc_ref, kc_ref, vc_ref, g_ref, yc_ref, cj, d, 0, emit)

    def pair(emit):
        def body(i, carry):
            chunk_step(ql_ref, kl_ref, vl_ref, gfl_ref, yl_ref, i, 0, lc, emit)
            chunk_step(ql_ref, kl_ref, vl_ref, gbl_ref, yl_ref, n_l - 1 - i, 1, lc, emit)
            return carry
        return body

    lax.fori_loop(0, n_l // 2, pair("first"), 0, unroll=SCAN_UNROLL)
    lax.fori_loop(n_l // 2, n_l, pair("second"), 0, unroll=SCAN_UNROLL)


def _ret_scan(decay, ctx_parts, lat_parts, *, n_batch, n_heads, ctx_out):
    qc, kc, vc, gfc, gbc = ctx_parts
    ql, kl, vl, gfl, gbl = lat_parts
    _, n_c, qk_w, c = kc.shape
    n_l = kl.shape[1]
    assert n_l % 2 == 0, "the latent walk is split into a first and a second half"
    dk = qk_w // n_heads
    dv = vl.shape[1] // n_heads
    lc, ll = n_c * c, n_l * c

    def seq_specs(seq, n_chunks):
        row = lambda w: pl.BlockSpec((seq, w), lambda h, b: (b, h))
        return [row(dk), pl.BlockSpec((None, n_chunks, dk, c), lambda h, b: (b, 0, h, 0)),
                row(dv), row(dv), row(dv)]

    out_specs = [pl.BlockSpec((ll, dv), lambda h, b: (b, h))]
    out_shape = [jax.ShapeDtypeStruct((n_batch * ll, n_heads * dv), BF16)]
    if ctx_out:
        out_specs.append(pl.BlockSpec((lc, dv), lambda h, b: (b, h)))
        out_shape.append(jax.ShapeDtypeStruct((n_batch * lc, n_heads * dv), BF16))
    res = pl.pallas_call(
        functools.partial(_ret_scan_kernel, ctx_out=ctx_out),
        grid=(n_heads, n_batch),
        in_specs=[pl.BlockSpec((None, 2, 1), lambda h, b: (h, 0, 0))] + seq_specs(lc, n_c) + seq_specs(ll, n_l),
        out_specs=out_specs,
        out_shape=out_shape,
        scratch_shapes=[
            pltpu.VMEM((2, c, c), F32),
            pltpu.VMEM((2, c, dk), F32),
            pltpu.VMEM((2, 8, c), F32),
            pltpu.VMEM((2, dk, dv), F32),
            pltpu.VMEM((lc + ll, dv), F32),
        ],
        compiler_params=_params(2),
        name="ret_scan",
    )(decay, qc, kc, vc, gfc, gbc, ql, kl, vl, gfl, gbl)
    return res if ctx_out else (res[0], None)


def _rope_tables(seq, dk):
    t = jnp.arange(seq)
    row = (t // GRID_W).astype(F32)
    col = (t % GRID_W).astype(F32)
    n_freq = dk // 4
    inv = ROPE_BASE ** (-jnp.arange(n_freq, dtype=F32) / n_freq)
    ang = jnp.concatenate([row[:, None] * inv, col[:, None] * inv], -1)
    cos, sin = jnp.cos(ang), jnp.sin(ang)
    return cos, sin, cos.T, sin.T


def kernel(x, c, ctx, c_ctx, ada_w, ada_b, ln1_g, ln1_b, ln2_g, ln2_b, ffn_w1, ffn_w2,
           a_w_in, a_b_in, a_ln_g, a_ln_b, a_w_s, a_b_s, a_w_out, b_w_in, b_decay, b_w_out):
    n_batch, seq, d = x.shape
    ctx_len = ctx.shape[1]
    depth = ada_w.shape[0]
    alpha = float((2 * depth) ** 0.25)
    n_heads = b_decay.shape[-1]
    dk = d // n_heads
    qk_w = n_heads * dk
    d_a = a_w_out.shape[1]
    n_groups = a_w_s.shape[1]

    mod = _modulation(jnp.concatenate([c, c_ctx[None]], axis=0), ada_w, ada_b)
    mod = mod.reshape(depth, n_batch + 1, 6, d)
    rope_tabs = _rope_tables(seq, dk)

    xl = x.reshape(n_batch * seq, d)
    xc = ctx.reshape(n_batch * ctx_len, d)
    row = lambda p: p.reshape(1, -1)

    for i in range(depth):
        ctx_out = i < depth - 1
        mod_l, mod_c = mod[i, :n_batch], mod[i, n_batch:]
        j = i // 2
        g1, b1 = row(ln1_g[i]), row(ln1_b[i])
        if i % 2 == 0:
            bs_rows = jnp.repeat(a_b_s[j].T, d_a // n_groups, axis=1)
            weights = (a_w_in[j].astype(BF16), row(a_b_in[j]), row(a_ln_g[j]), row(a_ln_b[j]),
                       a_w_s[j].astype(BF16), bs_rows, a_w_out[j].astype(BF16), g1, b1)
            xl = _gmlp(xl, mod_l, *weights, alpha=alpha, tm=GMLP_ROW_TILE)
            if ctx_out:
                xc = _gmlp(xc, mod_c, *weights, alpha=alpha, tm=GMLP_ROW_TILE)
            mix_l = mix_c = None
        else:
            w_in = b_w_in[j]
            wq = w_in[:, :qk_w].astype(BF16)
            wkt = w_in[:, qk_w:2 * qk_w].T.astype(BF16)
            wr = w_in[:, 2 * qk_w:].astype(BF16)
            w_out = b_w_out[j].astype(BF16)
            lat = _ret_proj(xl, mod_l, wq, wkt, wr, rope_tabs, n_batch=n_batch, n_heads=n_heads)
            cpt = _ret_proj(xc, mod_c, wq, wkt, wr, None, n_batch=n_batch, n_heads=n_heads)
            decay = b_decay[j].T.reshape(n_heads, 2, 1)
            y_l, y_c = _ret_scan(decay, cpt, lat, n_batch=n_batch, n_heads=n_heads, ctx_out=ctx_out)
            mix_l, mix_c = (y_l, w_out, g1, b1), (y_c, w_out, g1, b1)
        ffn_args = (ffn_w1[i].astype(BF16), ffn_w2[i].astype(BF16), row(ln2_g[i]), row(ln2_b[i]))
        xl = _ffn(xl, mod_l, *ffn_args, alpha=alpha, tm=FFN_ROW_TILE, mixer=mix_l)
        if ctx_out:
            xc = _ffn(xc, mod_c, *ffn_args, alpha=alpha, tm=FFN_ROW_TILE, mixer=mix_c)
    return xl.reshape(n_batch, seq, d)
```

```python
import functools
import math

import jax
import jax.numpy as jnp
from jax import lax
from jax.experimental import pallas as pl
from jax.experimental.pallas import tpu as pltpu

F32 = jnp.float32
BF16 = jnp.bfloat16

LN_EPS = 1e-5
GRID_W = 64
ROPE_BASE = 10000.0
RET_CHUNK = 256
SCAN_UNROLL = 4
SUB_TILE = 256
FFN_ROW_TILE = 1024
PROJ_ROW_TILE = 512
GMLP_ROW_TILE = 1024
GMLP_SUB_TILE = 256
GMLP_COL_CHUNK = 512
FFN_COL_CHUNK = 1024
VMEM_LIMIT = 56 * 1024 * 1024


def _params(n_axes, vmem=VMEM_LIMIT):
    return pltpu.CompilerParams(dimension_semantics=("arbitrary",) * n_axes, vmem_limit_bytes=vmem)


def _const_spec(shape):
    zeros = (0,) * len(shape)
    return pl.BlockSpec(shape, lambda *_: zeros, pipeline_mode=pl.Buffered(1))


def _mod_spec(d, tiles_per_row):
    return pl.BlockSpec((None, 6, d), lambda t: (t // tiles_per_row, 0, 0))


def _row_tiling(rows, n_mod_rows, tm):
    rows_per_mod = rows // n_mod_rows
    tm = min(tm, rows_per_mod)
    assert rows_per_mod % tm == 0 and tm % SUB_TILE == 0, (rows, n_mod_rows, tm)
    return tm, rows_per_mod // tm


def _layernorm(y, g, b):
    mu = jnp.mean(y, axis=-1, keepdims=True)
    yc = y - mu
    var = jnp.mean(yc * yc, axis=-1, keepdims=True)
    return yc * lax.rsqrt(var + LN_EPS) * g + b


def _modulate(x, shift, scale):
    return (x * (1.0 + scale) + shift).astype(BF16)


def _gelu(z):
    return 0.5 * z * (1.0 + lax.erf(z * math.sqrt(0.5)))


def _modulation_kernel(c_ref, w_ref, b_ref, o_ref):
    sc = c_ref[...]
    sc = sc * jax.nn.sigmoid(sc)
    o_ref[...] = jnp.dot(sc, w_ref[...], preferred_element_type=F32,
                         precision=lax.Precision.HIGHEST) + b_ref[...]


def _modulation(cc, ada_w, ada_b):
    depth, d, n = ada_w.shape
    nb = cc.shape[0]
    tn = 1536
    return pl.pallas_call(
        _modulation_kernel,
        grid=(depth, n // tn),
        in_specs=[
            pl.BlockSpec((nb, d), lambda i, j: (0, 0)),
            pl.BlockSpec((None, d, tn), lambda i, j: (i, 0, j)),
            pl.BlockSpec((None, 1, tn), lambda i, j: (i, 0, j)),
        ],
        out_specs=pl.BlockSpec((None, nb, tn), lambda i, j: (i, 0, j)),
        out_shape=jax.ShapeDtypeStruct((depth, nb, n), F32),
        compiler_params=_params(2),
        name="modulation",
    )(cc, ada_w, ada_b.reshape(depth, 1, n))


def _ffn_kernel(*refs, alpha, mixer_out):
    if mixer_out:
        y_ref, wo_ref, g1_ref, b1_ref, x_ref, mod_ref, w1_ref, w2_ref, g_ref, b_ref, o_ref = refs
    else:
        x_ref, mod_ref, w1_ref, w2_ref, g_ref, b_ref, o_ref = refs
    n_hidden = w1_ref.shape[1]
    n_sub = x_ref.shape[0] // SUB_TILE

    def prologue(s):
        rows = slice(s * SUB_TILE, (s + 1) * SUB_TILE)
        x = x_ref[rows, :]
        if mixer_out:
            mix = jnp.dot(y_ref[rows, :], wo_ref[...], preferred_element_type=F32)
            x = _layernorm(alpha * x + mod_ref[2:3, :] * mix, g1_ref[...], b1_ref[...])
        return x, _modulate(x, mod_ref[3:4, :], mod_ref[4:5, :])

    def epilogue(s, x, acc):
        y = alpha * x + mod_ref[5:6, :] * acc
        o_ref[s * SUB_TILE:(s + 1) * SUB_TILE, :] = _layernorm(y, g_ref[...], b_ref[...])

    chunks = list(range(0, n_hidden, FFN_COL_CHUNK))
    nxt = prologue(0)
    done = None
    for s in range(n_sub):
        x, h = nxt
        acc = None
        for k, c0 in enumerate(chunks):
            z = jnp.dot(h, w1_ref[:, c0:c0 + FFN_COL_CHUNK], preferred_element_type=F32)
            a = jnp.maximum(z, 0.0)
            a = (a * a).astype(BF16)
            part = jnp.dot(a, w2_ref[c0:c0 + FFN_COL_CHUNK, :], preferred_element_type=F32)
            acc = part if acc is None else acc + part
            if k == 0 and done is not None:
                epilogue(*done)
            if k == len(chunks) // 2 and s + 1 < n_sub:
                nxt = prologue(s + 1)
        done = (s, x, acc)
    epilogue(*done)


def _ffn(x, mod, w1, w2, ln_g, ln_b, *, alpha, tm, mixer=None):
    rows, d = x.shape
    f = w1.shape[1]
    tm, tiles_per_row = _row_tiling(rows, mod.shape[0], tm)
    in_specs = [
        pl.BlockSpec((tm, d), lambda t: (t, 0)),
        _mod_spec(d, tiles_per_row),
        _const_spec((d, f)),
        _const_spec((f, d)),
        _const_spec((1, d)),
        _const_spec((1, d)),
    ]
    args = [x, mod, w1, w2, ln_g, ln_b]
    if mixer is not None:
        v_w = mixer[0].shape[1]
        in_specs = [pl.BlockSpec((tm, v_w), lambda t: (t, 0)), _const_spec((v_w, d)),
                    _const_spec((1, d)), _const_spec((1, d))] + in_specs
        args = list(mixer) + args
    return pl.pallas_call(
        functools.partial(_ffn_kernel, alpha=alpha, mixer_out=mixer is not None),
        grid=(rows // tm,),
        in_specs=in_specs,
        out_specs=pl.BlockSpec((tm, d), lambda t: (t, 0)),
        out_shape=jax.ShapeDtypeStruct((rows, d), F32),
        compiler_params=_params(1),
        name="ffn" if mixer is None else "mix_ffn",
    )(*args)


def _gmlp_kernel(x_ref, mod_ref, win_ref, bin_ref, lng_ref, lnb_ref, ws_ref, bs_ref, wout_ref,
                 g_ref, b_ref, o_ref, s_ref, *, alpha):
    d_a = wout_ref.shape[0]
    n_groups, chunk, _ = ws_ref.shape
    cg = d_a // n_groups
    SUB_TILE = GMLP_SUB_TILE
    n_sub = x_ref.shape[0] // SUB_TILE

    def norm_stage(s):
        t0 = s * SUB_TILE
        x = x_ref[t0:t0 + SUB_TILE, :]
        h = _modulate(x, mod_ref[0:1, :], mod_ref[1:2, :])
        v = jnp.dot(h, win_ref[:, d_a:], preferred_element_type=F32) + bin_ref[:, d_a:]
        return x, h, _layernorm(_gelu(v), lng_ref[...], lnb_ref[...]).astype(BF16)

    def spatial_stage(s, v):
        t0 = s * SUB_TILE
        for r0 in range(0, SUB_TILE, chunk):
            for g in range(n_groups):
                s_ref[t0 + r0:t0 + r0 + chunk, g * cg:(g + 1) * cg] = (
                    jnp.dot(ws_ref[g], v[r0:r0 + chunk, g * cg:(g + 1) * cg], preferred_element_type=F32)
                    + bs_ref[:, g * cg:(g + 1) * cg])

    def gated_chunk(s, h, c0):
        cols = slice(c0, c0 + GMLP_COL_CHUNK)
        u = _gelu(jnp.dot(h, win_ref[:, cols], preferred_element_type=F32) + bin_ref[:, cols])
        return (u * s_ref[s * SUB_TILE:(s + 1) * SUB_TILE, cols]).astype(BF16)

    def out_chunk(acc, gated, c0):
        part = jnp.dot(gated, wout_ref[c0:c0 + GMLP_COL_CHUNK, :], preferred_element_type=F32)
        return part if acc is None else acc + part

    def epilogue(s, x, acc):
        y = alpha * x + mod_ref[2:3, :] * acc
        o_ref[s * SUB_TILE:(s + 1) * SUB_TILE, :] = _layernorm(y, g_ref[...], b_ref[...])

    chunks = list(range(0, d_a, GMLP_COL_CHUNK))
    x, h, v = norm_stage(0)
    spatial_stage(0, v)
    for s in range(n_sub):
        more = s + 1 < n_sub
        acc = None
        gated = gated_chunk(s, h, chunks[0])
        if more:
            x_next, h_next, v_next = norm_stage(s + 1)
        for c0 in chunks[1:]:
            gated_prev, prev_c0 = gated, c0 - GMLP_COL_CHUNK
            gated = gated_chunk(s, h, c0)
            acc = out_chunk(acc, gated_prev, prev_c0)
        if more:
            spatial_stage(s + 1, v_next)
        acc = out_chunk(acc, gated, chunks[-1])
        epilogue(s, x, acc)
        if more:
            x, h = x_next, h_next


def _gmlp(x, mod, w_in, b_in, ln_g, ln_b, w_s, b_s_rows, w_out, ln1_g, ln1_b, *, alpha, tm):
    rows, d = x.shape
    d_a = w_out.shape[0]
    n_groups, chunk, _ = w_s.shape
    tm, tiles_per_row = _row_tiling(rows, mod.shape[0], tm)
    return pl.pallas_call(
        functools.partial(_gmlp_kernel, alpha=alpha),
        grid=(rows // tm,),
        in_specs=[
            pl.BlockSpec((tm, d), lambda t: (t, 0)),
            _mod_spec(d, tiles_per_row),
            _const_spec((d, 2 * d_a)),
            _const_spec((1, 2 * d_a)),
            _const_spec((1, d_a)),
            _const_spec((1, d_a)),
            _const_spec((n_groups, chunk, chunk)),
            _const_spec((chunk, d_a)),
            _const_spec((d_a, d)),
            _const_spec((1, d)),
            _const_spec((1, d)),
        ],
        out_specs=pl.BlockSpec((tm, d), lambda t: (t, 0)),
        out_shape=jax.ShapeDtypeStruct((rows, d), F32),
        scratch_shapes=[pltpu.VMEM((tm, d_a), F32)],
        compiler_params=_params(1),
        name="gmlp",
    )(x, mod, w_in, b_in, ln_g, ln_b, w_s, b_s_rows, w_out, ln1_g, ln1_b)


def _rope_pairs(t1, t2, cos, sin):
    return t1 * cos - t2 * sin, t1 * sin + t2 * cos


def _ret_proj_kernel(*refs, rope, n_heads):
    if rope:
        (x_ref, mod_ref, wq_ref, wkt_ref, wr_ref, cos_ref, sin_ref, cost_ref, sint_ref,
         q_ref, kt_ref, v_ref, gf_ref, gb_ref) = refs
    else:
        x_ref, mod_ref, wq_ref, wkt_ref, wr_ref, q_ref, kt_ref, v_ref, gf_ref, gb_ref = refs
    qk_w = wq_ref.shape[1]
    dk = qk_w // n_heads
    half = dk // 2
    v_w = v_ref.shape[1]
    n_sub, _, c = kt_ref.shape

    def prologue(s):
        return _modulate(x_ref[s * c:(s + 1) * c, :], mod_ref[0:1, :], mod_ref[1:2, :])

    h = prologue(0)
    for s in range(n_sub):
        rows = slice(s * c, (s + 1) * c)
        q = jnp.dot(h, wq_ref[...], preferred_element_type=F32)
        kt = lax.dot_general(wkt_ref[...], h, (((1,), (1,)), ((), ())), preferred_element_type=F32)
        kt = kt * (dk ** -0.5)
        if rope:
            cos, sin = cos_ref[rows, :], sin_ref[rows, :]
            cost, sint = cost_ref[:, rows], sint_ref[:, rows]
            for hh in range(n_heads):
                a, b = hh * dk, hh * dk + half
                o1, o2 = _rope_pairs(q[:, a:b], q[:, b:b + half], cos, sin)
                q_ref[rows, a:b] = o1.astype(BF16)
                q_ref[rows, b:b + half] = o2.astype(BF16)
                o1, o2 = _rope_pairs(kt[a:b, :], kt[b:b + half, :], cost, sint)
                kt_ref[s, a:b, :] = o1.astype(BF16)
                kt_ref[s, b:b + half, :] = o2.astype(BF16)
        else:
            q_ref[rows, :] = q.astype(BF16)
            kt_ref[s] = kt.astype(BF16)
        v_ref[rows, :] = jnp.dot(h, wr_ref[:, :v_w], preferred_element_type=F32).astype(BF16)
        h_next = prologue(s + 1) if s + 1 < n_sub else None
        for j, ref in enumerate((gf_ref, gb_ref), start=1):
            gate = jnp.dot(h, wr_ref[:, j * v_w:(j + 1) * v_w], preferred_element_type=F32)
            ref[rows, :] = (gate * jax.nn.sigmoid(gate)).astype(BF16)
        h = h_next


def _ret_proj(x, mod, wq, wkt, wr, rope_tabs, *, n_batch, n_heads):
    rows, d = x.shape
    c = RET_CHUNK
    seq = rows // n_batch
    qk_w = wq.shape[1]
    v_w = wr.shape[1] // 3
    rope = rope_tabs is not None
    tm, tiles_per_row = _row_tiling(rows, n_batch if rope else mod.shape[0], PROJ_ROW_TILE)
    if rope:
        tiles_per_row *= n_batch // mod.shape[0]
    assert tm % c == 0
    tiles_per_seq = seq // tm
    in_specs = [
        pl.BlockSpec((tm, d), lambda t: (t, 0)),
        _mod_spec(d, tiles_per_row),
        _const_spec((d, qk_w)),
        _const_spec((qk_w, d)),
        _const_spec((d, 3 * v_w)),
    ]
    args = [x, mod, wq, wkt, wr]
    if rope:
        half = qk_w // n_heads // 2
        in_specs += [
            pl.BlockSpec((tm, half), lambda t: (t % tiles_per_seq, 0)),
            pl.BlockSpec((tm, half), lambda t: (t % tiles_per_seq, 0)),
            pl.BlockSpec((half, tm), lambda t: (0, t % tiles_per_seq)),
            pl.BlockSpec((half, tm), lambda t: (0, t % tiles_per_seq)),
        ]
        args += list(rope_tabs)
    row_spec = lambda w: pl.BlockSpec((tm, w), lambda t: (t, 0))
    return pl.pallas_call(
        functools.partial(_ret_proj_kernel, rope=rope, n_heads=n_heads),
        grid=(rows // tm,),
        in_specs=in_specs,
        out_specs=[
            row_spec(qk_w),
            pl.BlockSpec((tm // c, qk_w, c), lambda t: (t, 0, 0)),
            row_spec(v_w), row_spec(v_w), row_spec(v_w),
        ],
        out_shape=[
            jax.ShapeDtypeStruct((rows, qk_w), BF16),
            jax.ShapeDtypeStruct((rows // c, qk_w, c), BF16),
            jax.ShapeDtypeStruct((rows, v_w), BF16),
            jax.ShapeDtypeStruct((rows, v_w), BF16),
            jax.ShapeDtypeStruct((rows, v_w), BF16),
        ],
        compiler_params=_params(1),
        name="ret_proj_rope" if rope else "ret_proj",
    )(*args)


def _ret_scan_kernel(*refs, ctx_out):
    (dec_ref, qc_ref, kc_ref, vc_ref, gfc_ref, gbc_ref, ql_ref, kl_ref, vl_ref, gfl_ref, gbl_ref) = refs[:11]
    if ctx_out:
        yl_ref, yc_ref = refs[11:13]
        scratch = refs[13:]
    else:
        yl_ref, yc_ref = refs[11], None
        scratch = refs[12:]
    mask_ref, xi_ref, zeta_ref, s_ref, yacc_ref = scratch
    n_c, dk, c = kc_ref.shape
    n_l = kl_ref.shape[0]
    lc = n_c * c

    def log_gamma():
        return jnp.log1p(-jnp.exp(dec_ref[...]))

    @pl.when(pl.program_id(1) == 0)
    def _():
        lg = log_gamma()
        lgf, lgb = lg[0:1, :], lg[1:2, :]
        ii = lax.broadcasted_iota(jnp.int32, (c, c), 0).astype(F32)
        jj = lax.broadcasted_iota(jnp.int32, (c, c), 1).astype(F32)
        diff = ii - jj
        mask_ref[0] = jnp.where(diff >= 0, jnp.exp(lgf * jnp.maximum(diff, 0.0)), 0.0)
        mask_ref[1] = jnp.where(diff <= 0, jnp.exp(lgb * jnp.maximum(-diff, 0.0)), 0.0)
        pi = lax.broadcasted_iota(jnp.int32, (c, dk), 0).astype(F32)
        xi_ref[0] = jnp.exp(lgf * (pi + 1.0))
        xi_ref[1] = jnp.exp(lgb * (c - pi))
        pj = lax.broadcasted_iota(jnp.int32, (8, c), 1).astype(F32)
        zeta_ref[0] = jnp.exp(lgf * (c - 1.0 - pj))
        zeta_ref[1] = jnp.exp(lgb * pj)

    gamma_c = jnp.exp(log_gamma() * float(c))

    def chunk_step(q_ref, k_ref, v_ref, g_ref, y_ref, ci, d, row_off, emit):
        r0 = ci * c if isinstance(ci, int) else pl.multiple_of(ci * c, c)
        qc = q_ref[pl.ds(r0, c), :]
        kt = k_ref[ci]
        vc = v_ref[pl.ds(r0, c), :]
        state = s_ref[d]
        kz = (kt.astype(F32) * zeta_ref[d, 0:1, :]).astype(BF16)
        s_ref[d] = state * gamma_c[d:d + 1, :] + jnp.dot(kz, vc, preferred_element_type=F32)
        if emit is None:
            return
        scores = jnp.dot(qc, kt, preferred_element_type=F32)
        p = (scores * mask_ref[d]).astype(BF16)
        qx = (qc.astype(F32) * xi_ref[d]).astype(BF16)
        o = (jnp.dot(p, vc, preferred_element_type=F32)
             + jnp.dot(qx, state.astype(BF16), preferred_element_type=F32))
        mu = jnp.mean(o, axis=-1, keepdims=True)
        oc = o - mu
        var = jnp.mean(oc * oc, axis=-1, keepdims=True)
        gated = oc * lax.rsqrt(var + LN_EPS) * g_ref[pl.ds(r0, c), :].astype(F32)
        parked = pl.ds(row_off + r0, c)
        if emit == "first":
            yacc_ref[parked, :] = gated
        else:
            y_ref[pl.ds(r0, c), :] = (yacc_ref[parked, :] + gated).astype(BF16)

    s_ref[...] = jnp.zeros_like(s_ref)
    seen = set()
    for ci in range(n_c):
        for d, g_ref, cj in ((0, gfc_ref, ci), (1, gbc_ref, n_c - 1 - ci)):
            emit = None if not ctx_out else ("second" if cj in seen else "first")
            seen.add(cj)
            chunk_step(qc_ref, kc_ref, vc_ref, g_ref, yc_ref, cj, d, 0, emit)

    def pair(emit):
        def body(i, carry):
            chunk_step(ql_ref, kl_ref, vl_ref, gfl_ref, yl_ref, i, 0, lc, emit)
            chunk_step(ql_ref, kl_ref, vl_ref, gbl_ref, yl_ref, n_l - 1 - i, 1, lc, emit)
            return carry
        return body

    lax.fori_loop(0, n_l // 2, pair("first"), 0, unroll=SCAN_UNROLL)
    lax.fori_loop(n_l // 2, n_l, pair("second"), 0, unroll=SCAN_UNROLL)


def _ret_scan(decay, ctx_parts, lat_parts, *, n_batch, n_heads, ctx_out):
    qc, kc, vc, gfc, gbc = ctx_parts
    ql, kl, vl, gfl, gbl = lat_parts
    _, qk_w, c = kc.shape
    n_c, n_l = kc.shape[0] // n_batch, kl.shape[0] // n_batch
    assert n_l % 2 == 0, "the latent walk is split into a first and a second half"
    dk = qk_w // n_heads
    dv = vl.shape[1] // n_heads
    lc, ll = n_c * c, n_l * c

    def seq_specs(seq, n_chunks):
        row = lambda w: pl.BlockSpec((seq, w), lambda h, b: (b, h))
        return [row(dk), pl.BlockSpec((n_chunks, dk, c), lambda h, b: (b, h, 0)),
                row(dv), row(dv), row(dv)]

    out_specs = [pl.BlockSpec((ll, dv), lambda h, b: (b, h))]
    out_shape = [jax.ShapeDtypeStruct((n_batch * ll, n_heads * dv), BF16)]
    if ctx_out:
        out_specs.append(pl.BlockSpec((lc, dv), lambda h, b: (b, h)))
        out_shape.append(jax.ShapeDtypeStruct((n_batch * lc, n_heads * dv), BF16))
    res = pl.pallas_call(
        functools.partial(_ret_scan_kernel, ctx_out=ctx_out),
        grid=(n_heads, n_batch),
        in_specs=[pl.BlockSpec((None, 2, 1), lambda h, b: (h, 0, 0))] + seq_specs(lc, n_c) + seq_specs(ll, n_l),
        out_specs=out_specs,
        out_shape=out_shape,
        scratch_shapes=[
            pltpu.VMEM((2, c, c), F32),
            pltpu.VMEM((2, c, dk), F32),
            pltpu.VMEM((2, 8, c), F32),
            pltpu.VMEM((2, dk, dv), F32),
            pltpu.VMEM((lc + ll, dv), F32),
        ],
        compiler_params=_params(2),
        name="ret_scan",
    )(decay, qc, kc, vc, gfc, gbc, ql, kl, vl, gfl, gbl)
    return res if ctx_out else (res[0], None)


def _rope_tables(seq, dk):
    t = jnp.arange(seq)
    row = (t // GRID_W).astype(F32)
    col = (t % GRID_W).astype(F32)
    n_freq = dk // 4
    inv = ROPE_BASE ** (-jnp.arange(n_freq, dtype=F32) / n_freq)
    ang = jnp.concatenate([row[:, None] * inv, col[:, None] * inv], -1)
    cos, sin = jnp.cos(ang), jnp.sin(ang)
    return cos, sin, cos.T, sin.T


def kernel(x, c, ctx, c_ctx, ada_w, ada_b, ln1_g, ln1_b, ln2_g, ln2_b, ffn_w1, ffn_w2,
           a_w_in, a_b_in, a_ln_g, a_ln_b, a_w_s, a_b_s, a_w_out, b_w_in, b_decay, b_w_out):
    n_batch, seq, d = x.shape
    ctx_len = ctx.shape[1]
    depth = ada_w.shape[0]
    alpha = float((2 * depth) ** 0.25)
    n_heads = b_decay.shape[-1]
    dk = d // n_heads
    qk_w = n_heads * dk
    d_a = a_w_out.shape[1]
    n_groups = a_w_s.shape[1]

    mod = _modulation(jnp.concatenate([c, c_ctx[None]], axis=0), ada_w, ada_b)
    mod = mod.reshape(depth, n_batch + 1, 6, d)
    rope_tabs = _rope_tables(seq, dk)

    xl = x.reshape(n_batch * seq, d)
    xc = ctx.reshape(n_batch * ctx_len, d)
    row = lambda p: p.reshape(1, -1)

    for i in range(depth):
        ctx_out = i < depth - 1
        mod_l, mod_c = mod[i, :n_batch], mod[i, n_batch:]
        j = i // 2
        g1, b1 = row(ln1_g[i]), row(ln1_b[i])
        if i % 2 == 0:
            bs_rows = jnp.repeat(a_b_s[j].T, d_a // n_groups, axis=1)
            weights = (a_w_in[j].astype(BF16), row(a_b_in[j]), row(a_ln_g[j]), row(a_ln_b[j]),
                       a_w_s[j].astype(BF16), bs_rows, a_w_out[j].astype(BF16), g1, b1)
            xl = _gmlp(xl, mod_l, *weights, alpha=alpha, tm=GMLP_ROW_TILE)
            if ctx_out:
                xc = _gmlp(xc, mod_c, *weights, alpha=alpha, tm=GMLP_ROW_TILE)
            mix_l = mix_c = None
        else:
            w_in = b_w_in[j]
            wq = w_in[:, :qk_w].astype(BF16)
            wkt = w_in[:, qk_w:2 * qk_w].T.astype(BF16)
            wr = w_in[:, 2 * qk_w:].astype(BF16)
            w_out = b_w_out[j].astype(BF16)
            lat = _ret_proj(xl, mod_l, wq, wkt, wr, rope_tabs, n_batch=n_batch, n_heads=n_heads)
            cpt = _ret_proj(xc, mod_c, wq, wkt, wr, None, n_batch=n_batch, n_heads=n_heads)
            decay = b_decay[j].T.reshape(n_heads, 2, 1)
            y_l, y_c = _ret_scan(decay, cpt, lat, n_batch=n_batch, n_heads=n_heads, ctx_out=ctx_out)
            mix_l, mix_c = (y_l, w_out, g1, b1), (y_c, w_out, g1, b1)
        ffn_args = (ffn_w1[i].astype(BF16), ffn_w2[i].astype(BF16), row(ln2_g[i]), row(ln2_b[i]))
        xl = _ffn(xl, mod_l, *ffn_args, alpha=alpha, tm=FFN_ROW_TILE, mixer=mix_l)
        if ctx_out:
            xc = _ffn(xc, mod_c, *ffn_args, alpha=alpha, tm=FFN_ROW_TILE, mixer=mix_c)
    return xl.reshape(n_batch, seq, d)
```

```python
import functools
import math

import jax
import jax.numpy as jnp
from jax import lax
from jax.experimental import pallas as pl
from jax.experimental.pallas import tpu as pltpu

F32 = jnp.float32
BF16 = jnp.bfloat16

LN_EPS = 1e-5
GRID_W = 64
ROPE_BASE = 10000.0
RET_CHUNK = 256
SCAN_UNROLL = 4
SUB_TILE = 256
FFN_ROW_TILE = 1024
PROJ_ROW_TILE = 512
GMLP_ROW_TILE = 1024
GMLP_SUB_TILE = 256
GMLP_COL_CHUNK = 512
FFN_COL_CHUNK = 1024
VMEM_LIMIT = 56 * 1024 * 1024


def _params(n_axes, vmem=VMEM_LIMIT):
    return pltpu.CompilerParams(dimension_semantics=("arbitrary",) * n_axes, vmem_limit_bytes=vmem)


def _const_spec(shape):
    zeros = (0,) * len(shape)
    return pl.BlockSpec(shape, lambda *_: zeros, pipeline_mode=pl.Buffered(1))


def _mod_spec(d, tiles_per_row):
    return pl.BlockSpec((None, 6, d), lambda t: (t // tiles_per_row, 0, 0))


def _row_tiling(rows, n_mod_rows, tm):
    rows_per_mod = rows // n_mod_rows
    tm = min(tm, rows_per_mod)
    assert rows_per_mod % tm == 0 and tm % SUB_TILE == 0, (rows, n_mod_rows, tm)
    return tm, rows_per_mod // tm


def _layernorm(y, g, b):
    mu = jnp.mean(y, axis=-1, keepdims=True)
    yc = y - mu
    var = jnp.mean(yc * yc, axis=-1, keepdims=True)
    return yc * lax.rsqrt(var + LN_EPS) * g + b


def _modulate(x, shift, scale):
    return (x * (1.0 + scale) + shift).astype(BF16)


def _gelu(z):
    return 0.5 * z * (1.0 + lax.erf(z * math.sqrt(0.5)))


def _modulation_kernel(c_ref, w_ref, b_ref, o_ref):
    sc = c_ref[...]
    sc = sc * jax.nn.sigmoid(sc)
    o_ref[...] = jnp.dot(sc, w_ref[...], preferred_element_type=F32,
                         precision=lax.Precision.HIGHEST) + b_ref[...]


def _modulation(cc, ada_w, ada_b):
    depth, d, n = ada_w.shape
    nb = cc.shape[0]
    tn = 1536
    return pl.pallas_call(
        _modulation_kernel,
        grid=(depth, n // tn),
        in_specs=[
            pl.BlockSpec((nb, d), lambda i, j: (0, 0)),
            pl.BlockSpec((None, d, tn), lambda i, j: (i, 0, j)),
            pl.BlockSpec((None, 1, tn), lambda i, j: (i, 0, j)),
        ],
        out_specs=pl.BlockSpec((None, nb, tn), lambda i, j: (i, 0, j)),
        out_shape=jax.ShapeDtypeStruct((depth, nb, n), F32),
        compiler_params=_params(2),
        name="modulation",
    )(cc, ada_w, ada_b.reshape(depth, 1, n))


def _ffn_kernel(*refs, alpha, mixer_out):
    if mixer_out:
        y_ref, wo_ref, g1_ref, b1_ref, x_ref, mod_ref, w1_ref, w2_ref, g_ref, b_ref, o_ref = refs
    else:
        x_ref, mod_ref, w1_ref, w2_ref, g_ref, b_ref, o_ref = refs
    n_hidden = w1_ref.shape[1]
    n_sub = x_ref.shape[0] // SUB_TILE

    def prologue(s):
        rows = slice(s * SUB_TILE, (s + 1) * SUB_TILE)
        x = x_ref[rows, :]
        if mixer_out:
            mix = jnp.dot(y_ref[rows, :], wo_ref[...], preferred_element_type=F32)
            x = _layernorm(alpha * x + mod_ref[2:3, :] * mix, g1_ref[...], b1_ref[...])
        return x, _modulate(x, mod_ref[3:4, :], mod_ref[4:5, :])

    def epilogue(s, x, acc):
        y = alpha * x + mod_ref[5:6, :] * acc
        o_ref[s * SUB_TILE:(s + 1) * SUB_TILE, :] = _layernorm(y, g_ref[...], b_ref[...])

    chunks = list(range(0, n_hidden, FFN_COL_CHUNK))
    nxt = prologue(0)
    done = None
    for s in range(n_sub):
        x, h = nxt
        acc = None
        for k, c0 in enumerate(chunks):
            z = jnp.dot(h, w1_ref[:, c0:c0 + FFN_COL_CHUNK], preferred_element_type=F32)
            a = jnp.maximum(z, 0.0)
            a = (a * a).astype(BF16)
            part = jnp.dot(a, w2_ref[c0:c0 + FFN_COL_CHUNK, :], preferred_element_type=F32)
            acc = part if acc is None else acc + part
            if k == 0 and done is not None:
                epilogue(*done)
            if k == len(chunks) // 2 and s + 1 < n_sub:
                nxt = prologue(s + 1)
        done = (s, x, acc)
    epilogue(*done)


def _ffn(x, mod, w1, w2, ln_g, ln_b, *, alpha, tm, mixer=None):
    rows, d = x.shape
    f = w1.shape[1]
    tm, tiles_per_row = _row_tiling(rows, mod.shape[0], tm)
    in_specs = [
        pl.BlockSpec((tm, d), lambda t: (t, 0)),
        _mod_spec(d, tiles_per_row),
        _const_spec((d, f)),
        _const_spec((f, d)),
        _const_spec((1, d)),
        _const_spec((1, d)),
    ]
    args = [x, mod, w1, w2, ln_g, ln_b]
    if mixer is not None:
        v_w = mixer[0].shape[1]
        in_specs = [pl.BlockSpec((tm, v_w), lambda t: (t, 0)), _const_spec((v_w, d)),
                    _const_spec((1, d)), _const_spec((1, d))] + in_specs
        args = list(mixer) + args
    return pl.pallas_call(
        functools.partial(_ffn_kernel, alpha=alpha, mixer_out=mixer is not None),
        grid=(rows // tm,),
        in_specs=in_specs,
        out_specs=pl.BlockSpec((tm, d), lambda t: (t, 0)),
        out_shape=jax.ShapeDtypeStruct((rows, d), F32),
        compiler_params=_params(1),
        name="ffn" if mixer is None else "mix_ffn",
    )(*args)


def _gmlp_kernel(x_ref, mod_ref, win_ref, bin_ref, lng_ref, lnb_ref, ws_ref, bs_ref, wout_ref,
                 g_ref, b_ref, o_ref, s_ref, *, alpha):
    d_a = wout_ref.shape[0]
    n_groups, chunk, _ = ws_ref.shape
    cg = d_a // n_groups
    SUB_TILE = GMLP_SUB_TILE
    n_sub = x_ref.shape[0] // SUB_TILE

    def norm_stage(s):
        t0 = s * SUB_TILE
        x = x_ref[t0:t0 + SUB_TILE, :]
        h = _modulate(x, mod_ref[0:1, :], mod_ref[1:2, :])
        v = jnp.dot(h, win_ref[:, d_a:], preferred_element_type=F32) + bin_ref[:, d_a:]
        return x, h, _layernorm(_gelu(v), lng_ref[...], lnb_ref[...]).astype(BF16)

    def spatial_stage(s, v):
        t0 = s * SUB_TILE
        for r0 in range(0, SUB_TILE, chunk):
            for g in range(n_groups):
                s_ref[t0 + r0:t0 + r0 + chunk, g * cg:(g + 1) * cg] = (
                    jnp.dot(ws_ref[g], v[r0:r0 + chunk, g * cg:(g + 1) * cg], preferred_element_type=F32)
                    + bs_ref[:, g * cg:(g + 1) * cg])

    def gated_chunk(s, h, c0):
        cols = slice(c0, c0 + GMLP_COL_CHUNK)
        u = _gelu(jnp.dot(h, win_ref[:, cols], preferred_element_type=F32) + bin_ref[:, cols])
        return (u * s_ref[s * SUB_TILE:(s + 1) * SUB_TILE, cols]).astype(BF16)

    def out_chunk(acc, gated, c0):
        part = jnp.dot(gated, wout_ref[c0:c0 + GMLP_COL_CHUNK, :], preferred_element_type=F32)
        return part if acc is None else acc + part

    def epilogue(s, x, acc):
        y = alpha * x + mod_ref[2:3, :] * acc
        o_ref[s * SUB_TILE:(s + 1) * SUB_TILE, :] = _layernorm(y, g_ref[...], b_ref[...])

    chunks = list(range(0, d_a, GMLP_COL_CHUNK))
    x, h, v = norm_stage(0)
    spatial_stage(0, v)
    for s in range(n_sub):
        more = s + 1 < n_sub
        acc = None
        gated = gated_chunk(s, h, chunks[0])
        if more:
            x_next, h_next, v_next = norm_stage(s + 1)
        for c0 in chunks[1:]:
            gated_prev, prev_c0 = gated, c0 - GMLP_COL_CHUNK
            gated = gated_chunk(s, h, c0)
            acc = out_chunk(acc, gated_prev, prev_c0)
        if more:
            spatial_stage(s + 1, v_next)
        acc = out_chunk(acc, gated, chunks[-1])
        epilogue(s, x, acc)
        if more:
            x, h = x_next, h_next


def _gmlp(x, mod, w_in, b_in, ln_g, ln_b, w_s, b_s_rows, w_out, ln1_g, ln1_b, *, alpha, tm):
    rows, d = x.shape
    d_a = w_out.shape[0]
    n_groups, chunk, _ = w_s.shape
    tm, tiles_per_row = _row_tiling(rows, mod.shape[0], tm)
    return pl.pallas_call(
        functools.partial(_gmlp_kernel, alpha=alpha),
        grid=(rows // tm,),
        in_specs=[
            pl.BlockSpec((tm, d), lambda t: (t, 0)),
            _mod_spec(d, tiles_per_row),
            _const_spec((d, 2 * d_a)),
            _const_spec((1, 2 * d_a)),
            _const_spec((1, d_a)),
            _const_spec((1, d_a)),
            _const_spec((n_groups, chunk, chunk)),
            _const_spec((chunk, d_a)),
            _const_spec((d_a, d)),
            _const_spec((1, d)),
            _const_spec((1, d)),
        ],
        out_specs=pl.BlockSpec((tm, d), lambda t: (t, 0)),
        out_shape=jax.ShapeDtypeStruct((rows, d), F32),
        scratch_shapes=[pltpu.VMEM((tm, d_a), F32)],
        compiler_params=_params(1),
        name="gmlp",
    )(x, mod, w_in, b_in, ln_g, ln_b, w_s, b_s_rows, w_out, ln1_g, ln1_b)


def _rope_pairs(t1, t2, cos, sin):
    return t1 * cos - t2 * sin, t1 * sin + t2 * cos


def _ret_proj_kv_kernel(x_ref, mod_ref, wkt_ref, wv_ref, kt_ref, v_ref, *, n_heads):
    n_sub, qk_w, c = kt_ref.shape
    scale = (qk_w // n_heads) ** -0.5
    for s in range(n_sub):
        rows = slice(s * c, (s + 1) * c)
        h = _modulate(x_ref[rows, :], mod_ref[0:1, :], mod_ref[1:2, :])
        kt = lax.dot_general(wkt_ref[...], h, (((1,), (1,)), ((), ())), preferred_element_type=F32)
        kt_ref[s] = (kt * scale).astype(BF16)
        v_ref[rows, :] = jnp.dot(h, wv_ref[...], preferred_element_type=F32).astype(BF16)


def _ret_proj_kernel(*refs, rope, n_heads):
    if rope:
        (x_ref, mod_ref, wq_ref, wkt_ref, wr_ref, cos_ref, sin_ref, cost_ref, sint_ref,
         q_ref, kt_ref, v_ref, gf_ref, gb_ref) = refs
    else:
        x_ref, mod_ref, wq_ref, wkt_ref, wr_ref, q_ref, kt_ref, v_ref, gf_ref, gb_ref = refs
    qk_w = wq_ref.shape[1]
    dk = qk_w // n_heads
    half = dk // 2
    v_w = v_ref.shape[1]
    n_sub, _, c = kt_ref.shape

    def prologue(s):
        return _modulate(x_ref[s * c:(s + 1) * c, :], mod_ref[0:1, :], mod_ref[1:2, :])

    h = prologue(0)
    for s in range(n_sub):
        rows = slice(s * c, (s + 1) * c)
        q = jnp.dot(h, wq_ref[...], preferred_element_type=F32)
        kt = lax.dot_general(wkt_ref[...], h, (((1,), (1,)), ((), ())), preferred_element_type=F32)
        kt = kt * (dk ** -0.5)
        if rope:
            cos, sin = cos_ref[rows, :], sin_ref[rows, :]
            cost, sint = cost_ref[:, rows], sint_ref[:, rows]
            for hh in range(n_heads):
                a, b = hh * dk, hh * dk + half
                o1, o2 = _rope_pairs(q[:, a:b], q[:, b:b + half], cos, sin)
                q_ref[rows, a:b] = o1.astype(BF16)
                q_ref[rows, b:b + half] = o2.astype(BF16)
                o1, o2 = _rope_pairs(kt[a:b, :], kt[b:b + half, :], cost, sint)
                kt_ref[s, a:b, :] = o1.astype(BF16)
                kt_ref[s, b:b + half, :] = o2.astype(BF16)
        else:
            q_ref[rows, :] = q.astype(BF16)
            kt_ref[s] = kt.astype(BF16)
        v_ref[rows, :] = jnp.dot(h, wr_ref[:, :v_w], preferred_element_type=F32).astype(BF16)
        h_next = prologue(s + 1) if s + 1 < n_sub else None
        for j, ref in enumerate((gf_ref, gb_ref), start=1):
            gate = jnp.dot(h, wr_ref[:, j * v_w:(j + 1) * v_w], preferred_element_type=F32)
            ref[rows, :] = (gate * jax.nn.sigmoid(gate)).astype(BF16)
        h = h_next


def _ret_proj_kv(x, mod, wkt, wr, *, n_heads):
    rows, d = x.shape
    c = RET_CHUNK
    qk_w = wkt.shape[0]
    v_w = wr.shape[1] // 3
    tm, tiles_per_row = _row_tiling(rows, mod.shape[0], PROJ_ROW_TILE)
    kt, v = pl.pallas_call(
        functools.partial(_ret_proj_kv_kernel, n_heads=n_heads),
        grid=(rows // tm,),
        in_specs=[
            pl.BlockSpec((tm, d), lambda t: (t, 0)),
            _mod_spec(d, tiles_per_row),
            _const_spec((qk_w, d)),
            pl.BlockSpec((d, v_w), lambda t: (0, 0), pipeline_mode=pl.Buffered(1)),
        ],
        out_specs=[pl.BlockSpec((tm // c, qk_w, c), lambda t: (t, 0, 0)), pl.BlockSpec((tm, v_w), lambda t: (t, 0))],
        out_shape=[jax.ShapeDtypeStruct((rows // c, qk_w, c), BF16), jax.ShapeDtypeStruct((rows, v_w), BF16)],
        compiler_params=_params(1),
        name="ret_proj_kv",
    )(x, mod, wkt, wr)
    return None, kt, v, None, None


def _ret_proj(x, mod, wq, wkt, wr, rope_tabs, *, n_batch, n_heads):
    rows, d = x.shape
    c = RET_CHUNK
    seq = rows // n_batch
    qk_w = wq.shape[1]
    v_w = wr.shape[1] // 3
    rope = rope_tabs is not None
    tm, tiles_per_row = _row_tiling(rows, n_batch if rope else mod.shape[0], PROJ_ROW_TILE)
    if rope:
        tiles_per_row *= n_batch // mod.shape[0]
    assert tm % c == 0
    tiles_per_seq = seq // tm
    in_specs = [
        pl.BlockSpec((tm, d), lambda t: (t, 0)),
        _mod_spec(d, tiles_per_row),
        _const_spec((d, qk_w)),
        _const_spec((qk_w, d)),
        _const_spec((d, 3 * v_w)),
    ]
    args = [x, mod, wq, wkt, wr]
    if rope:
        half = qk_w // n_heads // 2
        in_specs += [
            pl.BlockSpec((tm, half), lambda t: (t % tiles_per_seq, 0)),
            pl.BlockSpec((tm, half), lambda t: (t % tiles_per_seq, 0)),
            pl.BlockSpec((half, tm), lambda t: (0, t % tiles_per_seq)),
            pl.BlockSpec((half, tm), lambda t: (0, t % tiles_per_seq)),
        ]
        args += list(rope_tabs)
    row_spec = lambda w: pl.BlockSpec((tm, w), lambda t: (t, 0))
    return pl.pallas_call(
        functools.partial(_ret_proj_kernel, rope=rope, n_heads=n_heads),
        grid=(rows // tm,),
        in_specs=in_specs,
        out_specs=[
            row_spec(qk_w),
            pl.BlockSpec((tm // c, qk_w, c), lambda t: (t, 0, 0)),
            row_spec(v_w), row_spec(v_w), row_spec(v_w),
        ],
        out_shape=[
            jax.ShapeDtypeStruct((rows, qk_w), BF16),
            jax.ShapeDtypeStruct((rows // c, qk_w, c), BF16),
            jax.ShapeDtypeStruct((rows, v_w), BF16),
            jax.ShapeDtypeStruct((rows, v_w), BF16),
            jax.ShapeDtypeStruct((rows, v_w), BF16),
        ],
        compiler_params=_params(1),
        name="ret_proj_rope" if rope else "ret_proj",
    )(*args)


def _ret_scan_kernel(*refs, ctx_out):
    if ctx_out:
        (dec_ref, qc_ref, kc_ref, vc_ref, gfc_ref, gbc_ref, ql_ref, kl_ref, vl_ref, gfl_ref, gbl_ref,
         yl_ref, yc_ref) = refs[:13]
        scratch = refs[13:]
    else:
        dec_ref, kc_ref, vc_ref, ql_ref, kl_ref, vl_ref, gfl_ref, gbl_ref, yl_ref = refs[:9]
        qc_ref = gfc_ref = gbc_ref = yc_ref = None
        scratch = refs[9:]
    mask_ref, xi_ref, zeta_ref, s_ref, yacc_ref = scratch
    n_c, dk, c = kc_ref.shape
    n_l = kl_ref.shape[0]
    lc = n_c * c

    def log_gamma():
        return jnp.log1p(-jnp.exp(dec_ref[...]))

    @pl.when(pl.program_id(1) == 0)
    def _():
        lg = log_gamma()
        lgf, lgb = lg[0:1, :], lg[1:2, :]
        ii = lax.broadcasted_iota(jnp.int32, (c, c), 0).astype(F32)
        jj = lax.broadcasted_iota(jnp.int32, (c, c), 1).astype(F32)
        diff = ii - jj
        mask_ref[0] = jnp.where(diff >= 0, jnp.exp(lgf * jnp.maximum(diff, 0.0)), 0.0)
        mask_ref[1] = jnp.where(diff <= 0, jnp.exp(lgb * jnp.maximum(-diff, 0.0)), 0.0)
        pi = lax.broadcasted_iota(jnp.int32, (c, dk), 0).astype(F32)
        xi_ref[0] = jnp.exp(lgf * (pi + 1.0)).astype(BF16)
        xi_ref[1] = jnp.exp(lgb * (c - pi)).astype(BF16)
        pj = lax.broadcasted_iota(jnp.int32, (dk, c), 1).astype(F32)
        zeta_ref[0] = jnp.exp(lgf * (c - 1.0 - pj)).astype(BF16)
        zeta_ref[1] = jnp.exp(lgb * pj).astype(BF16)

    gamma_c = jnp.exp(log_gamma() * float(c))

    def chunk_step(q_ref, k_ref, v_ref, g_ref, y_ref, ci, d, row_off, emit):
        r0 = ci * c if isinstance(ci, int) else pl.multiple_of(ci * c, c)
        kt = k_ref[ci]
        vc = v_ref[pl.ds(r0, c), :]
        state = s_ref[d]
        kz = kt * zeta_ref[d]
        s_ref[d] = state * gamma_c[d:d + 1, :] + jnp.dot(kz, vc, preferred_element_type=F32)
        if emit is None:
            return
        qc = q_ref[pl.ds(r0, c), :]
        scores = jnp.dot(qc, kt, preferred_element_type=F32)
        p = (scores * mask_ref[d]).astype(BF16)
        qx = qc * xi_ref[d]
        o = (jnp.dot(p, vc, preferred_element_type=F32)
             + jnp.dot(qx, state.astype(BF16), preferred_element_type=F32))
        mu = jnp.mean(o, axis=-1, keepdims=True)
        oc = o - mu
        var = jnp.mean(oc * oc, axis=-1, keepdims=True)
        gated = (oc * lax.rsqrt(var + LN_EPS)).astype(BF16) * g_ref[pl.ds(r0, c), :]
        parked = pl.ds(row_off + r0, c)
        if emit == "first":
            yacc_ref[parked, :] = gated
        else:
            y_ref[pl.ds(r0, c), :] = yacc_ref[parked, :] + gated

    s_ref[...] = jnp.zeros_like(s_ref)
    seen = set()
    for ci in range(n_c):
        for d, g_ref, cj in ((0, gfc_ref, ci), (1, gbc_ref, n_c - 1 - ci)):
            emit = None if not ctx_out else ("second" if cj in seen else "first")
            seen.add(cj)
            chunk_step(qc_ref, kc_ref, vc_ref, g_ref, yc_ref, cj, d, 0, emit)

    def pair(emit):
        def body(i, carry):
            chunk_step(ql_ref, kl_ref, vl_ref, gfl_ref, yl_ref, i, 0, lc, emit)
            chunk_step(ql_ref, kl_ref, vl_ref, gbl_ref, yl_ref, n_l - 1 - i, 1, lc, emit)
            return carry
        return body

    lax.fori_loop(0, n_l // 2, pair("first"), 0, unroll=SCAN_UNROLL)
    lax.fori_loop(n_l // 2, n_l, pair("second"), 0, unroll=SCAN_UNROLL)


def _ret_scan(decay, ctx_parts, lat_parts, *, n_batch, n_heads, ctx_out):
    qc, kc, vc, gfc, gbc = ctx_parts
    ql, kl, vl, gfl, gbl = lat_parts
    _, qk_w, c = kc.shape
    n_c, n_l = kc.shape[0] // n_batch, kl.shape[0] // n_batch
    assert n_l % 2 == 0, "the latent walk is split into a first and a second half"
    dk = qk_w // n_heads
    dv = vl.shape[1] // n_heads
    lc, ll = n_c * c, n_l * c

    def seq_specs(seq, n_chunks, kv_only=False):
        row = lambda w: pl.BlockSpec((seq, w), lambda h, b: (b, h))
        kt_spec = pl.BlockSpec((n_chunks, dk, c), lambda h, b: (b, h, 0))
        return [kt_spec, row(dv)] if kv_only else [row(dk), kt_spec, row(dv), row(dv), row(dv)]

    out_specs = [pl.BlockSpec((ll, dv), lambda h, b: (b, h))]
    out_shape = [jax.ShapeDtypeStruct((n_batch * ll, n_heads * dv), BF16)]
    if ctx_out:
        out_specs.append(pl.BlockSpec((lc, dv), lambda h, b: (b, h)))
        out_shape.append(jax.ShapeDtypeStruct((n_batch * lc, n_heads * dv), BF16))
    res = pl.pallas_call(
        functools.partial(_ret_scan_kernel, ctx_out=ctx_out),
        grid=(n_heads, n_batch),
        in_specs=([pl.BlockSpec((None, 2, 1), lambda h, b: (h, 0, 0))]
                  + seq_specs(lc, n_c, kv_only=not ctx_out) + seq_specs(ll, n_l)),
        out_specs=out_specs,
        out_shape=out_shape,
        scratch_shapes=[
            pltpu.VMEM((2, c, c), F32),
            pltpu.VMEM((2, c, dk), BF16),
            pltpu.VMEM((2, dk, c), BF16),
            pltpu.VMEM((2, dk, dv), F32),
            pltpu.VMEM((lc + ll, dv), BF16),
        ],
        compiler_params=_params(2),
        name="ret_scan",
    )(decay, *((qc, kc, vc, gfc, gbc) if ctx_out else (kc, vc)), ql, kl, vl, gfl, gbl)
    return res if ctx_out else (res[0], None)


def _rope_tables(seq, dk):
    t = jnp.arange(seq)
    row = (t // GRID_W).astype(F32)
    col = (t % GRID_W).astype(F32)
    n_freq = dk // 4
    inv = ROPE_BASE ** (-jnp.arange(n_freq, dtype=F32) / n_freq)
    ang = jnp.concatenate([row[:, None] * inv, col[:, None] * inv], -1)
    cos, sin = jnp.cos(ang), jnp.sin(ang)
    return cos, sin, cos.T, sin.T


def kernel(x, c, ctx, c_ctx, ada_w, ada_b, ln1_g, ln1_b, ln2_g, ln2_b, ffn_w1, ffn_w2,
           a_w_in, a_b_in, a_ln_g, a_ln_b, a_w_s, a_b_s, a_w_out, b_w_in, b_decay, b_w_out):
    n_batch, seq, d = x.shape
    ctx_len = ctx.shape[1]
    depth = ada_w.shape[0]
    alpha = float((2 * depth) ** 0.25)
    n_heads = b_decay.shape[-1]
    dk = d // n_heads
    qk_w = n_heads * dk
    d_a = a_w_out.shape[1]
    n_groups = a_w_s.shape[1]

    mod = _modulation(jnp.concatenate([c, c_ctx[None]], axis=0), ada_w, ada_b)
    mod = mod.reshape(depth, n_batch + 1, 6, d)
    rope_tabs = _rope_tables(seq, dk)

    xl = x.reshape(n_batch * seq, d)
    xc = ctx.reshape(n_batch * ctx_len, d)
    row = lambda p: p.reshape(1, -1)

    for i in range(depth):
        ctx_out = i < depth - 1
        mod_l, mod_c = mod[i, :n_batch], mod[i, n_batch:]
        j = i // 2
        g1, b1 = row(ln1_g[i]), row(ln1_b[i])
        if i % 2 == 0:
            bs_rows = jnp.repeat(a_b_s[j].T, d_a // n_groups, axis=1)
            weights = (a_w_in[j].astype(BF16), row(a_b_in[j]), row(a_ln_g[j]), row(a_ln_b[j]),
                       a_w_s[j].astype(BF16), bs_rows, a_w_out[j].astype(BF16), g1, b1)
            xl = _gmlp(xl, mod_l, *weights, alpha=alpha, tm=GMLP_ROW_TILE)
            if ctx_out:
                xc = _gmlp(xc, mod_c, *weights, alpha=alpha, tm=GMLP_ROW_TILE)
            mix_l = mix_c = None
        else:
            w_in = b_w_in[j]
            wq = w_in[:, :qk_w].astype(BF16)
            wkt = w_in[:, qk_w:2 * qk_w].T.astype(BF16)
            wr = w_in[:, 2 * qk_w:].astype(BF16)
            w_out = b_w_out[j].astype(BF16)
            lat = _ret_proj(xl, mod_l, wq, wkt, wr, rope_tabs, n_batch=n_batch, n_heads=n_heads)
            if ctx_out:
                cpt = _ret_proj(xc, mod_c, wq, wkt, wr, None, n_batch=n_batch, n_heads=n_heads)
            else:
                cpt = _ret_proj_kv(xc, mod_c, wkt, wr, n_heads=n_heads)
            decay = b_decay[j].T.reshape(n_heads, 2, 1)
            y_l, y_c = _ret_scan(decay, cpt, lat, n_batch=n_batch, n_heads=n_heads, ctx_out=ctx_out)
            mix_l, mix_c = (y_l, w_out, g1, b1), (y_c, w_out, g1, b1)
        ffn_args = (ffn_w1[i].astype(BF16), ffn_w2[i].astype(BF16), row(ln2_g[i]), row(ln2_b[i]))
        xl = _ffn(xl, mod_l, *ffn_args, alpha=alpha, tm=FFN_ROW_TILE, mixer=mix_l)
        if ctx_out:
            xc = _ffn(xc, mod_c, *ffn_args, alpha=alpha, tm=FFN_ROW_TILE, mixer=mix_c)
    return xl.reshape(n_batch, seq, d)
```

```python
import functools
import math

import jax
import jax.numpy as jnp
from jax import lax
from jax.experimental import pallas as pl
from jax.experimental.pallas import tpu as pltpu

F32 = jnp.float32
BF16 = jnp.bfloat16

LN_EPS = 1e-5
GRID_W = 64
ROPE_BASE = 10000.0
RET_CHUNK = 256
SCAN_UNROLL = 4
SUB_TILE = 256
FFN_ROW_TILE = 2048
MIX_FFN_ROW_TILE = 1024
PROJ_ROW_TILE = 512
GMLP_ROW_TILE = 1024
GMLP_SUB_TILE = 256
GMLP_COL_CHUNK = 512
FFN_COL_CHUNK = 1024
MOD_COL_TILE = 3072
V7X_VMEM_BYTES = 64 * 1024 * 1024
VMEM_LIMIT = V7X_VMEM_BYTES - 8 * 1024 * 1024


def _params(n_axes, vmem=VMEM_LIMIT):
    return pltpu.CompilerParams(dimension_semantics=("arbitrary",) * n_axes, vmem_limit_bytes=vmem)


def _const_spec(shape):
    zeros = (0,) * len(shape)
    return pl.BlockSpec(shape, lambda *_: zeros, pipeline_mode=pl.Buffered(1))


def _mod_spec(d, tiles_per_row):
    return pl.BlockSpec((None, 6, d), lambda t: (t // tiles_per_row, 0, 0))


def _row_tiling(rows, n_mod_rows, tm):
    rows_per_mod = rows // n_mod_rows
    tm = min(tm, rows_per_mod)
    assert rows_per_mod % tm == 0 and tm % SUB_TILE == 0, (rows, n_mod_rows, tm)
    return tm, rows_per_mod // tm


def _layernorm(y, g, b):
    mu = jnp.mean(y, axis=-1, keepdims=True)
    yc = y - mu
    var = jnp.mean(yc * yc, axis=-1, keepdims=True)
    return yc * lax.rsqrt(var + LN_EPS) * g + b


def _modulate(x, shift, scale):
    return (x * (1.0 + scale) + shift).astype(BF16)


def _gelu(z):
    return 0.5 * z * (1.0 + lax.erf(z * math.sqrt(0.5)))


def _modulation_kernel(c_ref, w_ref, b_ref, o_ref):
    sc = c_ref[...]
    sc = sc * jax.nn.sigmoid(sc)
    o_ref[...] = jnp.dot(sc, w_ref[...], preferred_element_type=F32,
                         precision=lax.Precision.HIGHEST) + b_ref[...]


def _modulation(cc, ada_w, ada_b):
    depth, d, n = ada_w.shape
    nb = cc.shape[0]
    tn = MOD_COL_TILE
    return pl.pallas_call(
        _modulation_kernel,
        grid=(depth, n // tn),
        in_specs=[
            pl.BlockSpec((nb, d), lambda i, j: (0, 0)),
            pl.BlockSpec((None, d, tn), lambda i, j: (i, 0, j)),
            pl.BlockSpec((None, 1, tn), lambda i, j: (i, 0, j)),
        ],
        out_specs=pl.BlockSpec((None, nb, tn), lambda i, j: (i, 0, j)),
        out_shape=jax.ShapeDtypeStruct((depth, nb, n), F32),
        compiler_params=_params(2),
        name="modulation",
    )(cc, ada_w, ada_b.reshape(depth, 1, n))


def _ffn_kernel(*refs, alpha, mixer_out):
    if mixer_out:
        y_ref, wo_ref, g1_ref, b1_ref, x_ref, mod_ref, w1_ref, w2_ref, g_ref, b_ref, o_ref = refs
    else:
        x_ref, mod_ref, w1_ref, w2_ref, g_ref, b_ref, o_ref = refs
    n_hidden = w1_ref.shape[1]
    n_sub = x_ref.shape[0] // SUB_TILE

    def prologue(s):
        rows = slice(s * SUB_TILE, (s + 1) * SUB_TILE)
        x = x_ref[rows, :]
        if mixer_out:
            mix = jnp.dot(y_ref[rows, :], wo_ref[...], preferred_element_type=F32)
            x = _layernorm(alpha * x + mod_ref[2:3, :] * mix, g1_ref[...], b1_ref[...])
        return x, _modulate(x, mod_ref[3:4, :], mod_ref[4:5, :])

    def epilogue(s, x, acc):
        y = alpha * x + mod_ref[5:6, :] * acc
        o_ref[s * SUB_TILE:(s + 1) * SUB_TILE, :] = _layernorm(y, g_ref[...], b_ref[...])

    chunks = list(range(0, n_hidden, FFN_COL_CHUNK))
    nxt = prologue(0)
    done = None
    for s in range(n_sub):
        x, h = nxt
        acc = None
        for k, c0 in enumerate(chunks):
            z = jnp.dot(h, w1_ref[:, c0:c0 + FFN_COL_CHUNK], preferred_element_type=F32)
            a = jnp.maximum(z, 0.0)
            a = (a * a).astype(BF16)
            part = jnp.dot(a, w2_ref[c0:c0 + FFN_COL_CHUNK, :], preferred_element_type=F32)
            acc = part if acc is None else acc + part
            if k == 0 and done is not None:
                epilogue(*done)
            if k == len(chunks) // 2 and s + 1 < n_sub:
                nxt = prologue(s + 1)
        done = (s, x, acc)
    epilogue(*done)


def _ffn(x, mod, w1, w2, ln_g, ln_b, *, alpha, tm, mixer=None):
    rows, d = x.shape
    f = w1.shape[1]
    tm, tiles_per_row = _row_tiling(rows, mod.shape[0], tm)
    in_specs = [
        pl.BlockSpec((tm, d), lambda t: (t, 0)),
        _mod_spec(d, tiles_per_row),
        _const_spec((d, f)),
        _const_spec((f, d)),
        _const_spec((1, d)),
        _const_spec((1, d)),
    ]
    args = [x, mod, w1, w2, ln_g, ln_b]
    if mixer is not None:
        v_w = mixer[0].shape[1]
        in_specs = [pl.BlockSpec((tm, v_w), lambda t: (t, 0)), _const_spec((v_w, d)),
                    _const_spec((1, d)), _const_spec((1, d))] + in_specs
        args = list(mixer) + args
    return pl.pallas_call(
        functools.partial(_ffn_kernel, alpha=alpha, mixer_out=mixer is not None),
        grid=(rows // tm,),
        in_specs=in_specs,
        out_specs=pl.BlockSpec((tm, d), lambda t: (t, 0)),
        out_shape=jax.ShapeDtypeStruct((rows, d), F32),
        compiler_params=_params(1),
        name="ffn" if mixer is None else "mix_ffn",
    )(*args)


def _gmlp_kernel(x_ref, mod_ref, win_ref, bin_ref, lng_ref, lnb_ref, ws_ref, bs_ref, wout_ref,
                 g_ref, b_ref, o_ref, s_ref, *, alpha):
    d_a = wout_ref.shape[0]
    n_groups, chunk, _ = ws_ref.shape
    cg = d_a // n_groups
    SUB_TILE = GMLP_SUB_TILE
    n_sub = x_ref.shape[0] // SUB_TILE

    def norm_stage(s):
        t0 = s * SUB_TILE
        x = x_ref[t0:t0 + SUB_TILE, :]
        h = _modulate(x, mod_ref[0:1, :], mod_ref[1:2, :])
        v = jnp.dot(h, win_ref[:, d_a:], preferred_element_type=F32) + bin_ref[:, d_a:]
        return x, h, _layernorm(_gelu(v), lng_ref[...], lnb_ref[...]).astype(BF16)

    def spatial_stage(s, v):
        t0 = s * SUB_TILE
        for r0 in range(0, SUB_TILE, chunk):
            for g in range(n_groups):
                s_ref[t0 + r0:t0 + r0 + chunk, g * cg:(g + 1) * cg] = (
                    jnp.dot(ws_ref[g], v[r0:r0 + chunk, g * cg:(g + 1) * cg], preferred_element_type=F32)
                    + bs_ref[:, g * cg:(g + 1) * cg])

    def gated_chunk(s, h, c0):
        cols = slice(c0, c0 + GMLP_COL_CHUNK)
        u = _gelu(jnp.dot(h, win_ref[:, cols], preferred_element_type=F32) + bin_ref[:, cols])
        return (u * s_ref[s * SUB_TILE:(s + 1) * SUB_TILE, cols]).astype(BF16)

    def out_chunk(acc, gated, c0):
        part = jnp.dot(gated, wout_ref[c0:c0 + GMLP_COL_CHUNK, :], preferred_element_type=F32)
        return part if acc is None else acc + part

    def epilogue(s, x, acc):
        y = alpha * x + mod_ref[2:3, :] * acc
        o_ref[s * SUB_TILE:(s + 1) * SUB_TILE, :] = _layernorm(y, g_ref[...], b_ref[...])

    chunks = list(range(0, d_a, GMLP_COL_CHUNK))
    x, h, v = norm_stage(0)
    spatial_stage(0, v)
    for s in range(n_sub):
        more = s + 1 < n_sub
        acc = None
        gated = gated_chunk(s, h, chunks[0])
        if more:
            x_next, h_next, v_next = norm_stage(s + 1)
        for c0 in chunks[1:]:
            gated_prev, prev_c0 = gated, c0 - GMLP_COL_CHUNK
            gated = gated_chunk(s, h, c0)
            acc = out_chunk(acc, gated_prev, prev_c0)
        if more:
            spatial_stage(s + 1, v_next)
        acc = out_chunk(acc, gated, chunks[-1])
        epilogue(s, x, acc)
        if more:
            x, h = x_next, h_next


def _gmlp(x, mod, w_in, b_in, ln_g, ln_b, w_s, b_s_rows, w_out, ln1_g, ln1_b, *, alpha, tm):
    rows, d = x.shape
    d_a = w_out.shape[0]
    n_groups, chunk, _ = w_s.shape
    tm, tiles_per_row = _row_tiling(rows, mod.shape[0], tm)
    return pl.pallas_call(
        functools.partial(_gmlp_kernel, alpha=alpha),
        grid=(rows // tm,),
        in_specs=[
            pl.BlockSpec((tm, d), lambda t: (t, 0)),
            _mod_spec(d, tiles_per_row),
            _const_spec((d, 2 * d_a)),
            _const_spec((1, 2 * d_a)),
            _const_spec((1, d_a)),
            _const_spec((1, d_a)),
            _const_spec((n_groups, chunk, chunk)),
            _const_spec((chunk, d_a)),
            _const_spec((d_a, d)),
            _const_spec((1, d)),
            _const_spec((1, d)),
        ],
        out_specs=pl.BlockSpec((tm, d), lambda t: (t, 0)),
        out_shape=jax.ShapeDtypeStruct((rows, d), F32),
        scratch_shapes=[pltpu.VMEM((tm, d_a), F32)],
        compiler_params=_params(1),
        name="gmlp",
    )(x, mod, w_in, b_in, ln_g, ln_b, w_s, b_s_rows, w_out, ln1_g, ln1_b)


def _rope_pairs(t1, t2, cos, sin):
    return t1 * cos - t2 * sin, t1 * sin + t2 * cos


def _ret_proj_kv_kernel(x_ref, mod_ref, wkt_ref, wv_ref, kt_ref, v_ref, *, n_heads):
    n_sub, qk_w, c = kt_ref.shape
    scale = (qk_w // n_heads) ** -0.5
    for s in range(n_sub):
        rows = slice(s * c, (s + 1) * c)
        h = _modulate(x_ref[rows, :], mod_ref[0:1, :], mod_ref[1:2, :])
        kt = lax.dot_general(wkt_ref[...], h, (((1,), (1,)), ((), ())), preferred_element_type=F32)
        kt_ref[s] = (kt * scale).astype(BF16)
        v_ref[rows, :] = jnp.dot(h, wv_ref[...], preferred_element_type=F32).astype(BF16)


def _ret_proj_kernel(*refs, rope, n_heads):
    if rope:
        (x_ref, mod_ref, wq_ref, wkt_ref, wr_ref, cos_ref, sin_ref, cost_ref, sint_ref,
         q_ref, kt_ref, v_ref, gf_ref, gb_ref) = refs
    else:
        x_ref, mod_ref, wq_ref, wkt_ref, wr_ref, q_ref, kt_ref, v_ref, gf_ref, gb_ref = refs
    qk_w = wq_ref.shape[1]
    dk = qk_w // n_heads
    half = dk // 2
    v_w = v_ref.shape[1]
    n_sub, _, c = kt_ref.shape

    def prologue(s):
        return _modulate(x_ref[s * c:(s + 1) * c, :], mod_ref[0:1, :], mod_ref[1:2, :])

    h = prologue(0)
    for s in range(n_sub):
        rows = slice(s * c, (s + 1) * c)
        q = jnp.dot(h, wq_ref[...], preferred_element_type=F32)
        kt = lax.dot_general(wkt_ref[...], h, (((1,), (1,)), ((), ())), preferred_element_type=F32)
        kt = kt * (dk ** -0.5)
        if rope:
            cos, sin = cos_ref[rows, :], sin_ref[rows, :]
            cost, sint = cost_ref[:, rows], sint_ref[:, rows]
            for hh in range(n_heads):
                a, b = hh * dk, hh * dk + half
                o1, o2 = _rope_pairs(q[:, a:b], q[:, b:b + half], cos, sin)
                q_ref[rows, a:b] = o1.astype(BF16)
                q_ref[rows, b:b + half] = o2.astype(BF16)
                o1, o2 = _rope_pairs(kt[a:b, :], kt[b:b + half, :], cost, sint)
                kt_ref[s, a:b, :] = o1.astype(BF16)
                kt_ref[s, b:b + half, :] = o2.astype(BF16)
        else:
            q_ref[rows, :] = q.astype(BF16)
            kt_ref[s] = kt.astype(BF16)
        v_ref[rows, :] = jnp.dot(h, wr_ref[:, :v_w], preferred_element_type=F32).astype(BF16)
        h_next = prologue(s + 1) if s + 1 < n_sub else None
        for j, ref in enumerate((gf_ref, gb_ref), start=1):
            gate = jnp.dot(h, wr_ref[:, j * v_w:(j + 1) * v_w], preferred_element_type=F32)
            ref[rows, :] = (gate * jax.nn.sigmoid(gate)).astype(BF16)
        h = h_next


def _ret_proj_kv(x, mod, wkt, wr, *, n_heads):
    rows, d = x.shape
    c = RET_CHUNK
    qk_w = wkt.shape[0]
    v_w = wr.shape[1] // 3
    tm, tiles_per_row = _row_tiling(rows, mod.shape[0], PROJ_ROW_TILE)
    kt, v = pl.pallas_call(
        functools.partial(_ret_proj_kv_kernel, n_heads=n_heads),
        grid=(rows // tm,),
        in_specs=[
            pl.BlockSpec((tm, d), lambda t: (t, 0)),
            _mod_spec(d, tiles_per_row),
            _const_spec((qk_w, d)),
            pl.BlockSpec((d, v_w), lambda t: (0, 0), pipeline_mode=pl.Buffered(1)),
        ],
        out_specs=[pl.BlockSpec((tm // c, qk_w, c), lambda t: (t, 0, 0)), pl.BlockSpec((tm, v_w), lambda t: (t, 0))],
        out_shape=[jax.ShapeDtypeStruct((rows // c, qk_w, c), BF16), jax.ShapeDtypeStruct((rows, v_w), BF16)],
        compiler_params=_params(1),
        name="ret_proj_kv",
    )(x, mod, wkt, wr)
    return None, kt, v, None, None


def _ret_proj(x, mod, wq, wkt, wr, rope_tabs, *, n_batch, n_heads):
    rows, d = x.shape
    c = RET_CHUNK
    seq = rows // n_batch
    qk_w = wq.shape[1]
    v_w = wr.shape[1] // 3
    rope = rope_tabs is not None
    tm, tiles_per_row = _row_tiling(rows, n_batch if rope else mod.shape[0], PROJ_ROW_TILE)
    if rope:
        tiles_per_row *= n_batch // mod.shape[0]
    assert tm % c == 0
    tiles_per_seq = seq // tm
    in_specs = [
        pl.BlockSpec((tm, d), lambda t: (t, 0)),
        _mod_spec(d, tiles_per_row),
        _const_spec((d, qk_w)),
        _const_spec((qk_w, d)),
        _const_spec((d, 3 * v_w)),
    ]
    args = [x, mod, wq, wkt, wr]
    if rope:
        half = qk_w // n_heads // 2
        in_specs += [
            pl.BlockSpec((tm, half), lambda t: (t % tiles_per_seq, 0)),
            pl.BlockSpec((tm, half), lambda t: (t % tiles_per_seq, 0)),
            pl.BlockSpec((half, tm), lambda t: (0, t % tiles_per_seq)),
            pl.BlockSpec((half, tm), lambda t: (0, t % tiles_per_seq)),
        ]
        args += list(rope_tabs)
    row_spec = lambda w: pl.BlockSpec((tm, w), lambda t: (t, 0))
    return pl.pallas_call(
        functools.partial(_ret_proj_kernel, rope=rope, n_heads=n_heads),
        grid=(rows // tm,),
        in_specs=in_specs,
        out_specs=[
            row_spec(qk_w),
            pl.BlockSpec((tm // c, qk_w, c), lambda t: (t, 0, 0)),
            row_spec(v_w), row_spec(v_w), row_spec(v_w),
        ],
        out_shape=[
            jax.ShapeDtypeStruct((rows, qk_w), BF16),
            jax.ShapeDtypeStruct((rows // c, qk_w, c), BF16),
            jax.ShapeDtypeStruct((rows, v_w), BF16),
            jax.ShapeDtypeStruct((rows, v_w), BF16),
            jax.ShapeDtypeStruct((rows, v_w), BF16),
        ],
        compiler_params=_params(1),
        name="ret_proj_rope" if rope else "ret_proj",
    )(*args)


def _ret_scan_kernel(*refs, ctx_out):
    if ctx_out:
        (dec_ref, qc_ref, kc_ref, vc_ref, gfc_ref, gbc_ref, ql_ref, kl_ref, vl_ref, gfl_ref, gbl_ref,
         yl_ref, yc_ref) = refs[:13]
        scratch = refs[13:]
    else:
        dec_ref, kc_ref, vc_ref, ql_ref, kl_ref, vl_ref, gfl_ref, gbl_ref, yl_ref = refs[:9]
        qc_ref = gfc_ref = gbc_ref = yc_ref = None
        scratch = refs[9:]
    mask_ref, xi_ref, zeta_ref, s_ref, yacc_ref = scratch
    n_c, dk, c = kc_ref.shape
    n_l = kl_ref.shape[0]
    lc = n_c * c

    def log_gamma():
        return jnp.log1p(-jnp.exp(dec_ref[...]))

    @pl.when(pl.program_id(1) == 0)
    def _():
        lg = log_gamma()
        lgf, lgb = lg[0:1, :], lg[1:2, :]
        ii = lax.broadcasted_iota(jnp.int32, (c, c), 0).astype(F32)
        jj = lax.broadcasted_iota(jnp.int32, (c, c), 1).astype(F32)
        diff = ii - jj
        mask_ref[0] = jnp.where(diff >= 0, jnp.exp(lgf * jnp.maximum(diff, 0.0)), 0.0)
        mask_ref[1] = jnp.where(diff <= 0, jnp.exp(lgb * jnp.maximum(-diff, 0.0)), 0.0)
        pi = lax.broadcasted_iota(jnp.int32, (c, dk), 0).astype(F32)
        xi_ref[0] = jnp.exp(lgf * (pi + 1.0)).astype(BF16)
        xi_ref[1] = jnp.exp(lgb * (c - pi)).astype(BF16)
        pj = lax.broadcasted_iota(jnp.int32, (dk, c), 1).astype(F32)
        zeta_ref[0] = jnp.exp(lgf * (c - 1.0 - pj)).astype(BF16)
        zeta_ref[1] = jnp.exp(lgb * pj).astype(BF16)

    gamma_c = jnp.exp(log_gamma() * float(c))

    def chunk_step(q_ref, k_ref, v_ref, g_ref, y_ref, ci, d, row_off, emit):
        r0 = ci * c if isinstance(ci, int) else pl.multiple_of(ci * c, c)
        kt = k_ref[ci]
        vc = v_ref[pl.ds(r0, c), :]
        state = s_ref[d]
        kz = kt * zeta_ref[d]
        s_ref[d] = state * gamma_c[d:d + 1, :] + jnp.dot(kz, vc, preferred_element_type=F32)
        if emit is None:
            return
        qc = q_ref[pl.ds(r0, c), :]
        scores = jnp.dot(qc, kt, preferred_element_type=F32)
        p = (scores * mask_ref[d]).astype(BF16)
        qx = qc * xi_ref[d]
        o = (jnp.dot(p, vc, preferred_element_type=F32)
             + jnp.dot(qx, state.astype(BF16), preferred_element_type=F32))
        mu = jnp.mean(o, axis=-1, keepdims=True)
        oc = o - mu
        var = jnp.mean(oc * oc, axis=-1, keepdims=True)
        gated = (oc * lax.rsqrt(var + LN_EPS)).astype(BF16) * g_ref[pl.ds(r0, c), :]
        parked = pl.ds(row_off + r0, c)
        if emit == "first":
            yacc_ref[parked, :] = gated
        else:
            y_ref[pl.ds(r0, c), :] = yacc_ref[parked, :] + gated

    s_ref[...] = jnp.zeros_like(s_ref)
    seen = set()
    for ci in range(n_c):
        for d, g_ref, cj in ((0, gfc_ref, ci), (1, gbc_ref, n_c - 1 - ci)):
            emit = None if not ctx_out else ("second" if cj in seen else "first")
            seen.add(cj)
            chunk_step(qc_ref, kc_ref, vc_ref, g_ref, yc_ref, cj, d, 0, emit)

    def pair(emit):
        def body(i, carry):
            chunk_step(ql_ref, kl_ref, vl_ref, gfl_ref, yl_ref, i, 0, lc, emit)
            chunk_step(ql_ref, kl_ref, vl_ref, gbl_ref, yl_ref, n_l - 1 - i, 1, lc, emit)
            return carry
        return body

    lax.fori_loop(0, n_l // 2, pair("first"), 0, unroll=SCAN_UNROLL)
    lax.fori_loop(n_l // 2, n_l, pair("second"), 0, unroll=SCAN_UNROLL)


def _ret_scan(decay, ctx_parts, lat_parts, *, n_batch, n_heads, ctx_out):
    qc, kc, vc, gfc, gbc = ctx_parts
    ql, kl, vl, gfl, gbl = lat_parts
    _, qk_w, c = kc.shape
    n_c, n_l = kc.shape[0] // n_batch, kl.shape[0] // n_batch
    assert n_l % 2 == 0, "the latent walk is split into a first and a second half"
    dk = qk_w // n_heads
    dv = vl.shape[1] // n_heads
    lc, ll = n_c * c, n_l * c

    def seq_specs(seq, n_chunks, kv_only=False):
        row = lambda w: pl.BlockSpec((seq, w), lambda h, b: (b, h))
        kt_spec = pl.BlockSpec((n_chunks, dk, c), lambda h, b: (b, h, 0))
        return [kt_spec, row(dv)] if kv_only else [row(dk), kt_spec, row(dv), row(dv), row(dv)]

    out_specs = [pl.BlockSpec((ll, dv), lambda h, b: (b, h))]
    out_shape = [jax.ShapeDtypeStruct((n_batch * ll, n_heads * dv), BF16)]
    if ctx_out:
        out_specs.append(pl.BlockSpec((lc, dv), lambda h, b: (b, h)))
        out_shape.append(jax.ShapeDtypeStruct((n_batch * lc, n_heads * dv), BF16))
    res = pl.pallas_call(
        functools.partial(_ret_scan_kernel, ctx_out=ctx_out),
        grid=(n_heads, n_batch),
        in_specs=([pl.BlockSpec((None, 2, 1), lambda h, b: (h, 0, 0))]
                  + seq_specs(lc, n_c, kv_only=not ctx_out) + seq_specs(ll, n_l)),
        out_specs=out_specs,
        out_shape=out_shape,
        scratch_shapes=[
            pltpu.VMEM((2, c, c), F32),
            pltpu.VMEM((2, c, dk), BF16),
            pltpu.VMEM((2, dk, c), BF16),
            pltpu.VMEM((2, dk, dv), F32),
            pltpu.VMEM((lc + ll, dv), BF16),
        ],
        compiler_params=_params(2),
        name="ret_scan",
    )(decay, *((qc, kc, vc, gfc, gbc) if ctx_out else (kc, vc)), ql, kl, vl, gfl, gbl)
    return res if ctx_out else (res[0], None)


def _rope_tables(seq, dk):
    t = jnp.arange(seq)
    row = (t // GRID_W).astype(F32)
    col = (t % GRID_W).astype(F32)
    n_freq = dk // 4
    inv = ROPE_BASE ** (-jnp.arange(n_freq, dtype=F32) / n_freq)
    ang = jnp.concatenate([row[:, None] * inv, col[:, None] * inv], -1)
    cos, sin = jnp.cos(ang), jnp.sin(ang)
    return cos, sin, cos.T, sin.T


def kernel(x, c, ctx, c_ctx, ada_w, ada_b, ln1_g, ln1_b, ln2_g, ln2_b, ffn_w1, ffn_w2,
           a_w_in, a_b_in, a_ln_g, a_ln_b, a_w_s, a_b_s, a_w_out, b_w_in, b_decay, b_w_out):
    n_batch, seq, d = x.shape
    ctx_len = ctx.shape[1]
    depth = ada_w.shape[0]
    alpha = float((2 * depth) ** 0.25)
    n_heads = b_decay.shape[-1]
    dk = d // n_heads
    qk_w = n_heads * dk
    d_a = a_w_out.shape[1]
    n_groups = a_w_s.shape[1]

    mod = _modulation(jnp.concatenate([c, c_ctx[None]], axis=0), ada_w, ada_b)
    mod = mod.reshape(depth, n_batch + 1, 6, d)
    rope_tabs = _rope_tables(seq, dk)

    xl = x.reshape(n_batch * seq, d)
    xc = ctx.reshape(n_batch * ctx_len, d)
    row = lambda p: p.reshape(1, -1)

    for i in range(depth):
        ctx_out = i < depth - 1
        mod_l, mod_c = mod[i, :n_batch], mod[i, n_batch:]
        j = i // 2
        g1, b1 = row(ln1_g[i]), row(ln1_b[i])
        if i % 2 == 0:
            bs_rows = jnp.repeat(a_b_s[j].T, d_a // n_groups, axis=1)
            weights = (a_w_in[j].astype(BF16), row(a_b_in[j]), row(a_ln_g[j]), row(a_ln_b[j]),
                       a_w_s[j].astype(BF16), bs_rows, a_w_out[j].astype(BF16), g1, b1)
            xl = _gmlp(xl, mod_l, *weights, alpha=alpha, tm=GMLP_ROW_TILE)
            if ctx_out:
                xc = _gmlp(xc, mod_c, *weights, alpha=alpha, tm=GMLP_ROW_TILE)
            mix_l = mix_c = None
        else:
            w_in = b_w_in[j]
            wq = w_in[:, :qk_w].astype(BF16)
            wkt = w_in[:, qk_w:2 * qk_w].T.astype(BF16)
            wr = w_in[:, 2 * qk_w:].astype(BF16)
            w_out = b_w_out[j].astype(BF16)
            lat = _ret_proj(xl, mod_l, wq, wkt, wr, rope_tabs, n_batch=n_batch, n_heads=n_heads)
            if ctx_out:
                cpt = _ret_proj(xc, mod_c, wq, wkt, wr, None, n_batch=n_batch, n_heads=n_heads)
            else:
                cpt = _ret_proj_kv(xc, mod_c, wkt, wr, n_heads=n_heads)
            decay = b_decay[j].T.reshape(n_heads, 2, 1)
            y_l, y_c = _ret_scan(decay, cpt, lat, n_batch=n_batch, n_heads=n_heads, ctx_out=ctx_out)
            mix_l, mix_c = (y_l, w_out, g1, b1), (y_c, w_out, g1, b1)
        ffn_args = (ffn_w1[i].astype(BF16), ffn_w2[i].astype(BF16), row(ln2_g[i]), row(ln2_b[i]))
        ffn_tile = FFN_ROW_TILE if mix_l is None else MIX_FFN_ROW_TILE
        xl = _ffn(xl, mod_l, *ffn_args, alpha=alpha, tm=ffn_tile, mixer=mix_l)
        if ctx_out:
            xc = _ffn(xc, mod_c, *ffn_args, alpha=alpha, tm=ffn_tile, mixer=mix_c)
    return xl.reshape(n_batch, seq, d)
```

```python
import functools
import math

import jax
import jax.numpy as jnp
from jax import lax
from jax.experimental import pallas as pl
from jax.experimental.pallas import tpu as pltpu

F32 = jnp.float32
BF16 = jnp.bfloat16

LN_EPS = 1e-5
GRID_W = 64
ROPE_BASE = 10000.0
RET_CHUNK = 256
SCAN_UNROLL = 4
SUB_TILE = 256
FFN_ROW_TILE = 1024
PROJ_ROW_TILE = 512
GMLP_ROW_TILE = 1024
GMLP_SUB_TILE = 256
GMLP_COL_CHUNK = 1024
FFN_COL_CHUNK = 1024
VMEM_LIMIT = 56 * 1024 * 1024


def _params(n_axes, vmem=VMEM_LIMIT):
    return pltpu.CompilerParams(dimension_semantics=("arbitrary",) * n_axes, vmem_limit_bytes=vmem)


def _const_spec(shape):
    zeros = (0,) * len(shape)
    return pl.BlockSpec(shape, lambda *_: zeros, pipeline_mode=pl.Buffered(1))


def _mod_spec(d, tiles_per_row):
    return pl.BlockSpec((None, 6, d), lambda t: (t // tiles_per_row, 0, 0))


def _row_tiling(rows, n_mod_rows, tm):
    rows_per_mod = rows // n_mod_rows
    tm = min(tm, rows_per_mod)
    assert rows_per_mod % tm == 0 and tm % SUB_TILE == 0, (rows, n_mod_rows, tm)
    return tm, rows_per_mod // tm


def _layernorm(y, g, b):
    mu = jnp.mean(y, axis=-1, keepdims=True)
    yc = y - mu
    var = jnp.mean(yc * yc, axis=-1, keepdims=True)
    return yc * lax.rsqrt(var + LN_EPS) * g + b


def _modulate(x, shift, scale):
    return (x * (1.0 + scale) + shift).astype(BF16)


def _gelu(z):
    return 0.5 * z * (1.0 + lax.erf(z * math.sqrt(0.5)))


def _modulation_kernel(c_ref, w_ref, b_ref, o_ref):
    sc = c_ref[...]
    sc = sc * jax.nn.sigmoid(sc)
    o_ref[...] = jnp.dot(sc, w_ref[...], preferred_element_type=F32,
                         precision=lax.Precision.HIGHEST) + b_ref[...]


def _modulation(cc, ada_w, ada_b):
    depth, d, n = ada_w.shape
    nb = cc.shape[0]
    tn = 1536
    return pl.pallas_call(
        _modulation_kernel,
        grid=(depth, n // tn),
        in_specs=[
            pl.BlockSpec((nb, d), lambda i, j: (0, 0)),
            pl.BlockSpec((None, d, tn), lambda i, j: (i, 0, j)),
            pl.BlockSpec((None, 1, tn), lambda i, j: (i, 0, j)),
        ],
        out_specs=pl.BlockSpec((None, nb, tn), lambda i, j: (i, 0, j)),
        out_shape=jax.ShapeDtypeStruct((depth, nb, n), F32),
        compiler_params=_params(2),
        name="modulation",
    )(cc, ada_w, ada_b.reshape(depth, 1, n))


def _ffn_kernel(*refs, alpha, mixer_out):
    if mixer_out:
        y_ref, wo_ref, g1_ref, b1_ref, x_ref, mod_ref, w1_ref, w2_ref, g_ref, b_ref, o_ref = refs
    else:
        x_ref, mod_ref, w1_ref, w2_ref, g_ref, b_ref, o_ref = refs
    n_hidden = w1_ref.shape[1]
    n_sub = x_ref.shape[0] // SUB_TILE

    def prologue(s):
        rows = slice(s * SUB_TILE, (s + 1) * SUB_TILE)
        x = x_ref[rows, :]
        if mixer_out:
            mix = jnp.dot(y_ref[rows, :], wo_ref[...], preferred_element_type=F32)
            x = _layernorm(alpha * x + mod_ref[2:3, :] * mix, g1_ref[...], b1_ref[...])
        return x, _modulate(x, mod_ref[3:4, :], mod_ref[4:5, :])

    def epilogue(s, x, acc):
        y = alpha * x + mod_ref[5:6, :] * acc
        o_ref[s * SUB_TILE:(s + 1) * SUB_TILE, :] = _layernorm(y, g_ref[...], b_ref[...])

    chunks = list(range(0, n_hidden, FFN_COL_CHUNK))
    nxt = prologue(0)
    done = None
    for s in range(n_sub):
        x, h = nxt
        acc = None
        for k, c0 in enumerate(chunks):
            z = jnp.dot(h, w1_ref[:, c0:c0 + FFN_COL_CHUNK], preferred_element_type=F32)
            a = jnp.maximum(z, 0.0)
            a = (a * a).astype(BF16)
            part = jnp.dot(a, w2_ref[c0:c0 + FFN_COL_CHUNK, :], preferred_element_type=F32)
            acc = part if acc is None else acc + part
            if k == 0 and done is not None:
                epilogue(*done)
            if k == len(chunks) // 2 and s + 1 < n_sub:
                nxt = prologue(s + 1)
        done = (s, x, acc)
    epilogue(*done)


def _ffn(x, mod, w1, w2, ln_g, ln_b, *, alpha, tm, mixer=None):
    rows, d = x.shape
    f = w1.shape[1]
    tm, tiles_per_row = _row_tiling(rows, mod.shape[0], tm)
    in_specs = [
        pl.BlockSpec((tm, d), lambda t: (t, 0)),
        _mod_spec(d, tiles_per_row),
        _const_spec((d, f)),
        _const_spec((f, d)),
        _const_spec((1, d)),
        _const_spec((1, d)),
    ]
    args = [x, mod, w1, w2, ln_g, ln_b]
    if mixer is not None:
        v_w = mixer[0].shape[1]
        in_specs = [pl.BlockSpec((tm, v_w), lambda t: (t, 0)), _const_spec((v_w, d)),
                    _const_spec((1, d)), _const_spec((1, d))] + in_specs
        args = list(mixer) + args
    return pl.pallas_call(
        functools.partial(_ffn_kernel, alpha=alpha, mixer_out=mixer is not None),
        grid=(rows // tm,),
        in_specs=in_specs,
        out_specs=pl.BlockSpec((tm, d), lambda t: (t, 0)),
        out_shape=jax.ShapeDtypeStruct((rows, d), F32),
        compiler_params=_params(1),
        name="ffn" if mixer is None else "mix_ffn",
    )(*args)


def _gmlp_kernel(x_ref, mod_ref, win_ref, bin_ref, lng_ref, lnb_ref, ws_ref, bs_ref, wout_ref,
                 g_ref, b_ref, o_ref, s_ref, vbuf_ref, *, alpha):
    d_a = wout_ref.shape[0]
    n_groups, chunk, _ = ws_ref.shape
    cg = d_a // n_groups
    SUB_TILE = GMLP_SUB_TILE
    n_sub = x_ref.shape[0] // SUB_TILE

    def modulated(s):
        x = x_ref[s * SUB_TILE:(s + 1) * SUB_TILE, :]
        return x, _modulate(x, mod_ref[0:1, :], mod_ref[1:2, :])

    def v_chunk(h, k):
        lo = k * GMLP_COL_CHUNK
        cols = slice(d_a + lo, d_a + lo + GMLP_COL_CHUNK)
        act = _gelu(jnp.dot(h, win_ref[:, cols], preferred_element_type=F32) + bin_ref[:, cols])
        vbuf_ref[:, lo:lo + GMLP_COL_CHUNK] = act
        return jnp.sum(act, axis=-1, keepdims=True), jnp.sum(act * act, axis=-1, keepdims=True)

    def spatial_stage(s, stats):
        total = sum(st[0] for st in stats)
        total_sq = sum(st[1] for st in stats)
        mean = total * (1.0 / d_a)
        var = total_sq * (1.0 / d_a) - mean * mean
        rstd = lax.rsqrt(var + LN_EPS)
        t0 = s * SUB_TILE
        for g in range(n_groups):
            cols = slice(g * cg, (g + 1) * cg)
            v = ((vbuf_ref[:, cols] - mean) * rstd * lng_ref[:, cols] + lnb_ref[:, cols]).astype(BF16)
            for r0 in range(0, SUB_TILE, chunk):
                s_ref[t0 + r0:t0 + r0 + chunk, cols] = (
                    jnp.dot(ws_ref[g], v[r0:r0 + chunk, :], preferred_element_type=F32) + bs_ref[:, cols])

    def gated_chunk(s, h, k):
        cols = slice(k * GMLP_COL_CHUNK, (k + 1) * GMLP_COL_CHUNK)
        u = _gelu(jnp.dot(h, win_ref[:, cols], preferred_element_type=F32) + bin_ref[:, cols])
        return (u * s_ref[s * SUB_TILE:(s + 1) * SUB_TILE, cols]).astype(BF16)

    def out_chunk(acc, gated, k):
        rows = slice(k * GMLP_COL_CHUNK, (k + 1) * GMLP_COL_CHUNK)
        part = jnp.dot(gated, wout_ref[rows, :], preferred_element_type=F32)
        return part if acc is None else acc + part

    def epilogue(s, x, acc):
        y = alpha * x + mod_ref[2:3, :] * acc
        o_ref[s * SUB_TILE:(s + 1) * SUB_TILE, :] = _layernorm(y, g_ref[...], b_ref[...])

    n_chunks = d_a // GMLP_COL_CHUNK
    x, h = modulated(0)
    spatial_stage(0, [v_chunk(h, k) for k in range(n_chunks)])
    for s in range(n_sub):
        more = s + 1 < n_sub
        if more:
            x_next, h_next = modulated(s + 1)
        stats = []
        acc = None
        gated = gated_chunk(s, h, 0)
        for k in range(1, n_chunks):
            if more:
                stats.append(v_chunk(h_next, k - 1))
            gated_prev = gated
            gated = gated_chunk(s, h, k)
            acc = out_chunk(acc, gated_prev, k - 1)
        if more:
            stats.append(v_chunk(h_next, n_chunks - 1))
        acc = out_chunk(acc, gated, n_chunks - 1)
        if more:
            spatial_stage(s + 1, stats)
        epilogue(s, x, acc)
        if more:
            x, h = x_next, h_next


def _gmlp(x, mod, w_in, b_in, ln_g, ln_b, w_s, b_s_rows, w_out, ln1_g, ln1_b, *, alpha, tm):
    rows, d = x.shape
    d_a = w_out.shape[0]
    n_groups, chunk, _ = w_s.shape
    tm, tiles_per_row = _row_tiling(rows, mod.shape[0], tm)
    return pl.pallas_call(
        functools.partial(_gmlp_kernel, alpha=alpha),
        grid=(rows // tm,),
        in_specs=[
            pl.BlockSpec((tm, d), lambda t: (t, 0)),
            _mod_spec(d, tiles_per_row),
            _const_spec((d, 2 * d_a)),
            _const_spec((1, 2 * d_a)),
            _const_spec((1, d_a)),
            _const_spec((1, d_a)),
            _const_spec((n_groups, chunk, chunk)),
            _const_spec((chunk, d_a)),
            _const_spec((d_a, d)),
            _const_spec((1, d)),
            _const_spec((1, d)),
        ],
        out_specs=pl.BlockSpec((tm, d), lambda t: (t, 0)),
        out_shape=jax.ShapeDtypeStruct((rows, d), F32),
        scratch_shapes=[pltpu.VMEM((tm, d_a), F32),
                        pltpu.VMEM((GMLP_SUB_TILE, d_a), F32)],
        compiler_params=_params(1),
        name="gmlp",
    )(x, mod, w_in, b_in, ln_g, ln_b, w_s, b_s_rows, w_out, ln1_g, ln1_b)


def _rope_pairs(t1, t2, cos, sin):
    return t1 * cos - t2 * sin, t1 * sin + t2 * cos


def _ret_proj_kv_kernel(x_ref, mod_ref, wkt_ref, wv_ref, kt_ref, v_ref, *, n_heads):
    n_sub, qk_w, c = kt_ref.shape
    scale = (qk_w // n_heads) ** -0.5
    for s in range(n_sub):
        rows = slice(s * c, (s + 1) * c)
        h = _modulate(x_ref[rows, :], mod_ref[0:1, :], mod_ref[1:2, :])
        kt = lax.dot_general(wkt_ref[...], h, (((1,), (1,)), ((), ())), preferred_element_type=F32)
        kt_ref[s] = (kt * scale).astype(BF16)
        v_ref[rows, :] = jnp.dot(h, wv_ref[...], preferred_element_type=F32).astype(BF16)


def _ret_proj_kernel(*refs, rope, n_heads):
    if rope:
        (x_ref, mod_ref, wq_ref, wkt_ref, wr_ref, cos_ref, sin_ref, cost_ref, sint_ref,
         q_ref, kt_ref, v_ref, gf_ref, gb_ref) = refs
    else:
        x_ref, mod_ref, wq_ref, wkt_ref, wr_ref, q_ref, kt_ref, v_ref, gf_ref, gb_ref = refs
    qk_w = wq_ref.shape[1]
    dk = qk_w // n_heads
    half = dk // 2
    v_w = v_ref.shape[1]
    n_sub, _, c = kt_ref.shape

    def prologue(s):
        return _modulate(x_ref[s * c:(s + 1) * c, :], mod_ref[0:1, :], mod_ref[1:2, :])

    h = prologue(0)
    for s in range(n_sub):
        rows = slice(s * c, (s + 1) * c)
        q = jnp.dot(h, wq_ref[...], preferred_element_type=F32)
        kt = lax.dot_general(wkt_ref[...], h, (((1,), (1,)), ((), ())), preferred_element_type=F32)
        kt = kt * (dk ** -0.5)
        if rope:
            cos, sin = cos_ref[rows, :], sin_ref[rows, :]
            cost, sint = cost_ref[:, rows], sint_ref[:, rows]
            for hh in range(n_heads):
                a, b = hh * dk, hh * dk + half
                o1, o2 = _rope_pairs(q[:, a:b], q[:, b:b + half], cos, sin)
                q_ref[rows, a:b] = o1.astype(BF16)
                q_ref[rows, b:b + half] = o2.astype(BF16)
                o1, o2 = _rope_pairs(kt[a:b, :], kt[b:b + half, :], cost, sint)
                kt_ref[s, a:b, :] = o1.astype(BF16)
                kt_ref[s, b:b + half, :] = o2.astype(BF16)
        else:
            q_ref[rows, :] = q.astype(BF16)
            kt_ref[s] = kt.astype(BF16)
        v_ref[rows, :] = jnp.dot(h, wr_ref[:, :v_w], preferred_element_type=F32).astype(BF16)
        h_next = prologue(s + 1) if s + 1 < n_sub else None
        for j, ref in enumerate((gf_ref, gb_ref), start=1):
            gate = jnp.dot(h, wr_ref[:, j * v_w:(j + 1) * v_w], preferred_element_type=F32)
            ref[rows, :] = (gate * jax.nn.sigmoid(gate)).astype(BF16)
        h = h_next


def _ret_proj_kv(x, mod, wkt, wr, *, n_heads):
    rows, d = x.shape
    c = RET_CHUNK
    qk_w = wkt.shape[0]
    v_w = wr.shape[1] // 3
    tm, tiles_per_row = _row_tiling(rows, mod.shape[0], PROJ_ROW_TILE)
    kt, v = pl.pallas_call(
        functools.partial(_ret_proj_kv_kernel, n_heads=n_heads),
        grid=(rows // tm,),
        in_specs=[
            pl.BlockSpec((tm, d), lambda t: (t, 0)),
            _mod_spec(d, tiles_per_row),
            _const_spec((qk_w, d)),
            pl.BlockSpec((d, v_w), lambda t: (0, 0), pipeline_mode=pl.Buffered(1)),
        ],
        out_specs=[pl.BlockSpec((tm // c, qk_w, c), lambda t: (t, 0, 0)), pl.BlockSpec((tm, v_w), lambda t: (t, 0))],
        out_shape=[jax.ShapeDtypeStruct((rows // c, qk_w, c), BF16), jax.ShapeDtypeStruct((rows, v_w), BF16)],
        compiler_params=_params(1),
        name="ret_proj_kv",
    )(x, mod, wkt, wr)
    return None, kt, v, None, None


def _ret_proj(x, mod, wq, wkt, wr, rope_tabs, *, n_batch, n_heads):
    rows, d = x.shape
    c = RET_CHUNK
    seq = rows // n_batch
    qk_w = wq.shape[1]
    v_w = wr.shape[1] // 3
    rope = rope_tabs is not None
    tm, tiles_per_row = _row_tiling(rows, n_batch if rope else mod.shape[0], PROJ_ROW_TILE)
    if rope:
        tiles_per_row *= n_batch // mod.shape[0]
    assert tm % c == 0
    tiles_per_seq = seq // tm
    in_specs = [
        pl.BlockSpec((tm, d), lambda t: (t, 0)),
        _mod_spec(d, tiles_per_row),
        _const_spec((d, qk_w)),
        _const_spec((qk_w, d)),
        _const_spec((d, 3 * v_w)),
    ]
    args = [x, mod, wq, wkt, wr]
    if rope:
        half = qk_w // n_heads // 2
        in_specs += [
            pl.BlockSpec((tm, half), lambda t: (t % tiles_per_seq, 0)),
            pl.BlockSpec((tm, half), lambda t: (t % tiles_per_seq, 0)),
            pl.BlockSpec((half, tm), lambda t: (0, t % tiles_per_seq)),
            pl.BlockSpec((half, tm), lambda t: (0, t % tiles_per_seq)),
        ]
        args += list(rope_tabs)
    row_spec = lambda w: pl.BlockSpec((tm, w), lambda t: (t, 0))
    return pl.pallas_call(
        functools.partial(_ret_proj_kernel, rope=rope, n_heads=n_heads),
        grid=(rows // tm,),
        in_specs=in_specs,
        out_specs=[
            row_spec(qk_w),
            pl.BlockSpec((tm // c, qk_w, c), lambda t: (t, 0, 0)),
            row_spec(v_w), row_spec(v_w), row_spec(v_w),
        ],
        out_shape=[
            jax.ShapeDtypeStruct((rows, qk_w), BF16),
            jax.ShapeDtypeStruct((rows // c, qk_w, c), BF16),
            jax.ShapeDtypeStruct((rows, v_w), BF16),
            jax.ShapeDtypeStruct((rows, v_w), BF16),
            jax.ShapeDtypeStruct((rows, v_w), BF16),
        ],
        compiler_params=_params(1),
        name="ret_proj_rope" if rope else "ret_proj",
    )(*args)


def _ret_scan_kernel(*refs, ctx_out):
    if ctx_out:
        (dec_ref, qc_ref, kc_ref, vc_ref, gfc_ref, gbc_ref, ql_ref, kl_ref, vl_ref, gfl_ref, gbl_ref,
         yl_ref, yc_ref) = refs[:13]
        scratch = refs[13:]
    else:
        dec_ref, kc_ref, vc_ref, ql_ref, kl_ref, vl_ref, gfl_ref, gbl_ref, yl_ref = refs[:9]
        qc_ref = gfc_ref = gbc_ref = yc_ref = None
        scratch = refs[9:]
    mask_ref, xi_ref, zeta_ref, s_ref, yacc_ref = scratch
    n_c, dk, c = kc_ref.shape
    n_l = kl_ref.shape[0]
    lc = n_c * c

    def log_gamma():
        return jnp.log1p(-jnp.exp(dec_ref[...]))

    @pl.when(pl.program_id(1) == 0)
    def _():
        lg = log_gamma()
        lgf, lgb = lg[0:1, :], lg[1:2, :]
        ii = lax.broadcasted_iota(jnp.int32, (c, c), 0).astype(F32)
        jj = lax.broadcasted_iota(jnp.int32, (c, c), 1).astype(F32)
        diff = ii - jj
        mask_ref[0] = jnp.where(diff >= 0, jnp.exp(lgf * jnp.maximum(diff, 0.0)), 0.0)
        mask_ref[1] = jnp.where(diff <= 0, jnp.exp(lgb * jnp.maximum(-diff, 0.0)), 0.0)
        pi = lax.broadcasted_iota(jnp.int32, (c, dk), 0).astype(F32)
        xi_ref[0] = jnp.exp(lgf * (pi + 1.0)).astype(BF16)
        xi_ref[1] = jnp.exp(lgb * (c - pi)).astype(BF16)
        pj = lax.broadcasted_iota(jnp.int32, (dk, c), 1).astype(F32)
        zeta_ref[0] = jnp.exp(lgf * (c - 1.0 - pj)).astype(BF16)
        zeta_ref[1] = jnp.exp(lgb * pj).astype(BF16)

    gamma_c = jnp.exp(log_gamma() * float(c))

    def chunk_step(q_ref, k_ref, v_ref, g_ref, y_ref, ci, d, row_off, emit):
        r0 = ci * c if isinstance(ci, int) else pl.multiple_of(ci * c, c)
        kt = k_ref[ci]
        vc = v_ref[pl.ds(r0, c), :]
        state = s_ref[d]
        kz = kt * zeta_ref[d]
        s_ref[d] = state * gamma_c[d:d + 1, :] + jnp.dot(kz, vc, preferred_element_type=F32)
        if emit is None:
            return
        qc = q_ref[pl.ds(r0, c), :]
        scores = jnp.dot(qc, kt, preferred_element_type=F32)
        p = (scores * mask_ref[d]).astype(BF16)
        qx = qc * xi_ref[d]
        o = (jnp.dot(p, vc, preferred_element_type=F32)
             + jnp.dot(qx, state.astype(BF16), preferred_element_type=F32))
        mu = jnp.mean(o, axis=-1, keepdims=True)
        oc = o - mu
        var = jnp.mean(oc * oc, axis=-1, keepdims=True)
        gated = (oc * lax.rsqrt(var + LN_EPS)).astype(BF16) * g_ref[pl.ds(r0, c), :]
        parked = pl.ds(row_off + r0, c)
        if emit == "first":
            yacc_ref[parked, :] = gated
        else:
            y_ref[pl.ds(r0, c), :] = yacc_ref[parked, :] + gated

    s_ref[...] = jnp.zeros_like(s_ref)
    seen = set()
    for ci in range(n_c):
        for d, g_ref, cj in ((0, gfc_ref, ci), (1, gbc_ref, n_c - 1 - ci)):
            emit = None if not ctx_out else ("second" if cj in seen else "first")
            seen.add(cj)
            chunk_step(qc_ref, kc_ref, vc_ref, g_ref, yc_ref, cj, d, 0, emit)

    def pair(emit):
        def body(i, carry):
            chunk_step(ql_ref, kl_ref, vl_ref, gfl_ref, yl_ref, i, 0, lc, emit)
            chunk_step(ql_ref, kl_ref, vl_ref, gbl_ref, yl_ref, n_l - 1 - i, 1, lc, emit)
            return carry
        return body

    lax.fori_loop(0, n_l // 2, pair("first"), 0, unroll=SCAN_UNROLL)
    lax.fori_loop(n_l // 2, n_l, pair("second"), 0, unroll=SCAN_UNROLL)


def _ret_scan(decay, ctx_parts, lat_parts, *, n_batch, n_heads, ctx_out):
    qc, kc, vc, gfc, gbc = ctx_parts
    ql, kl, vl, gfl, gbl = lat_parts
    _, qk_w, c = kc.shape
    n_c, n_l = kc.shape[0] // n_batch, kl.shape[0] // n_batch
    assert n_l % 2 == 0, "the latent walk is split into a first and a second half"
    dk = qk_w // n_heads
    dv = vl.shape[1] // n_heads
    lc, ll = n_c * c, n_l * c

    def seq_specs(seq, n_chunks, kv_only=False):
        row = lambda w: pl.BlockSpec((seq, w), lambda h, b: (b, h))
        kt_spec = pl.BlockSpec((n_chunks, dk, c), lambda h, b: (b, h, 0))
        return [kt_spec, row(dv)] if kv_only else [row(dk), kt_spec, row(dv), row(dv), row(dv)]

    out_specs = [pl.BlockSpec((ll, dv), lambda h, b: (b, h))]
    out_shape = [jax.ShapeDtypeStruct((n_batch * ll, n_heads * dv), BF16)]
    if ctx_out:
        out_specs.append(pl.BlockSpec((lc, dv), lambda h, b: (b, h)))
        out_shape.append(jax.ShapeDtypeStruct((n_batch * lc, n_heads * dv), BF16))
    res = pl.pallas_call(
        functools.partial(_ret_scan_kernel, ctx_out=ctx_out),
        grid=(n_heads, n_batch),
        in_specs=([pl.BlockSpec((None, 2, 1), lambda h, b: (h, 0, 0))]
                  + seq_specs(lc, n_c, kv_only=not ctx_out) + seq_specs(ll, n_l)),
        out_specs=out_specs,
        out_shape=out_shape,
        scratch_shapes=[
            pltpu.VMEM((2, c, c), F32),
            pltpu.VMEM((2, c, dk), BF16),
            pltpu.VMEM((2, dk, c), BF16),
            pltpu.VMEM((2, dk, dv), F32),
            pltpu.VMEM((lc + ll, dv), BF16),
        ],
        compiler_params=_params(2),
        name="ret_scan",
    )(decay, *((qc, kc, vc, gfc, gbc) if ctx_out else (kc, vc)), ql, kl, vl, gfl, gbl)
    return res if ctx_out else (res[0], None)


def _rope_tables(seq, dk):
    t = jnp.arange(seq)
    row = (t // GRID_W).astype(F32)
    col = (t % GRID_W).astype(F32)
    n_freq = dk // 4
    inv = ROPE_BASE ** (-jnp.arange(n_freq, dtype=F32) / n_freq)
    ang = jnp.concatenate([row[:, None] * inv, col[:, None] * inv], -1)
    cos, sin = jnp.cos(ang), jnp.sin(ang)
    return cos, sin, cos.T, sin.T


def kernel(x, c, ctx, c_ctx, ada_w, ada_b, ln1_g, ln1_b, ln2_g, ln2_b, ffn_w1, ffn_w2,
           a_w_in, a_b_in, a_ln_g, a_ln_b, a_w_s, a_b_s, a_w_out, b_w_in, b_decay, b_w_out):
    n_batch, seq, d = x.shape
    ctx_len = ctx.shape[1]
    depth = ada_w.shape[0]
    alpha = float((2 * depth) ** 0.25)
    n_heads = b_decay.shape[-1]
    dk = d // n_heads
    qk_w = n_heads * dk
    d_a = a_w_out.shape[1]
    n_groups = a_w_s.shape[1]

    mod = _modulation(jnp.concatenate([c, c_ctx[None]], axis=0), ada_w, ada_b)
    mod = mod.reshape(depth, n_batch + 1, 6, d)
    rope_tabs = _rope_tables(seq, dk)

    xl = x.reshape(n_batch * seq, d)
    xc = ctx.reshape(n_batch * ctx_len, d)
    row = lambda p: p.reshape(1, -1)

    for i in range(depth):
        ctx_out = i < depth - 1
        mod_l, mod_c = mod[i, :n_batch], mod[i, n_batch:]
        j = i // 2
        g1, b1 = row(ln1_g[i]), row(ln1_b[i])
        if i % 2 == 0:
            bs_rows = jnp.repeat(a_b_s[j].T, d_a // n_groups, axis=1)
            weights = (a_w_in[j].astype(BF16), row(a_b_in[j]), row(a_ln_g[j]), row(a_ln_b[j]),
                       a_w_s[j].astype(BF16), bs_rows, a_w_out[j].astype(BF16), g1, b1)
            xl = _gmlp(xl, mod_l, *weights, alpha=alpha, tm=GMLP_ROW_TILE)
            if ctx_out:
                xc = _gmlp(xc, mod_c, *weights, alpha=alpha, tm=GMLP_ROW_TILE)
            mix_l = mix_c = None
        else:
            w_in = b_w_in[j]
            wq = w_in[:, :qk_w].astype(BF16)
            wkt = w_in[:, qk_w:2 * qk_w].T.astype(BF16)
            wr = w_in[:, 2 * qk_w:].astype(BF16)
            w_out = b_w_out[j].astype(BF16)
            lat = _ret_proj(xl, mod_l, wq, wkt, wr, rope_tabs, n_batch=n_batch, n_heads=n_heads)
            if ctx_out:
                cpt = _ret_proj(xc, mod_c, wq, wkt, wr, None, n_batch=n_batch, n_heads=n_heads)
            else:
                cpt = _ret_proj_kv(xc, mod_c, wkt, wr, n_heads=n_heads)
            decay = b_decay[j].T.reshape(n_heads, 2, 1)
            y_l, y_c = _ret_scan(decay, cpt, lat, n_batch=n_batch, n_heads=n_heads, ctx_out=ctx_out)
            mix_l, mix_c = (y_l, w_out, g1, b1), (y_c, w_out, g1, b1)
        ffn_args = (ffn_w1[i].astype(BF16), ffn_w2[i].astype(BF16), row(ln2_g[i]), row(ln2_b[i]))
        xl = _ffn(xl, mod_l, *ffn_args, alpha=alpha, tm=FFN_ROW_TILE, mixer=mix_l)
        if ctx_out:
            xc = _ffn(xc, mod_c, *ffn_args, alpha=alpha, tm=FFN_ROW_TILE, mixer=mix_c)
    return xl.reshape(n_batch, seq, d)
```

```python
import functools
import math

import jax
import jax.numpy as jnp
from jax import lax
from jax.experimental import pallas as pl
from jax.experimental.pallas import tpu as pltpu

F32 = jnp.float32
BF16 = jnp.bfloat16

LN_EPS = 1e-5
GRID_W = 64
ROPE_BASE = 10000.0
RET_CHUNK = 256
SCAN_UNROLL = 4
SUB_TILE = 256
FFN_ROW_TILE = 1024
PROJ_ROW_TILE = 512
GMLP_ROW_TILE = 1024
GMLP_SUB_TILE = 256
GMLP_COL_CHUNK = 1024
FFN_COL_CHUNK = 1024
VMEM_LIMIT = 56 * 1024 * 1024


def _params(n_axes, vmem=VMEM_LIMIT, independent=False):
    semantics = ("parallel" if independent else "arbitrary",) * n_axes
    return pltpu.CompilerParams(dimension_semantics=semantics, vmem_limit_bytes=vmem)


def _const_spec(shape):
    zeros = (0,) * len(shape)
    return pl.BlockSpec(shape, lambda *_: zeros, pipeline_mode=pl.Buffered(1))


def _mod_spec(d, tiles_per_row):
    return pl.BlockSpec((None, 6, d), lambda t: (t // tiles_per_row, 0, 0))


def _row_tiling(rows, n_mod_rows, tm):
    rows_per_mod = rows // n_mod_rows
    tm = min(tm, rows_per_mod)
    assert rows_per_mod % tm == 0 and tm % SUB_TILE == 0, (rows, n_mod_rows, tm)
    return tm, rows_per_mod // tm


def _layernorm(y, g, b):
    mu = jnp.mean(y, axis=-1, keepdims=True)
    yc = y - mu
    var = jnp.mean(yc * yc, axis=-1, keepdims=True)
    return yc * lax.rsqrt(var + LN_EPS) * g + b


def _modulate(x, shift, scale):
    return (x * (1.0 + scale) + shift).astype(BF16)


def _gelu(z):
    return 0.5 * z * (1.0 + lax.erf(z * math.sqrt(0.5)))


def _modulation_kernel(c_ref, w_ref, b_ref, o_ref):
    sc = c_ref[...]
    sc = sc * jax.nn.sigmoid(sc)
    o_ref[...] = jnp.dot(sc, w_ref[...], preferred_element_type=F32,
                         precision=lax.Precision.HIGHEST) + b_ref[...]


def _modulation(cc, ada_w, ada_b):
    depth, d, n = ada_w.shape
    nb = cc.shape[0]
    tn = 1536
    return pl.pallas_call(
        _modulation_kernel,
        grid=(depth, n // tn),
        in_specs=[
            pl.BlockSpec((nb, d), lambda i, j: (0, 0)),
            pl.BlockSpec((None, d, tn), lambda i, j: (i, 0, j)),
            pl.BlockSpec((None, 1, tn), lambda i, j: (i, 0, j)),
        ],
        out_specs=pl.BlockSpec((None, nb, tn), lambda i, j: (i, 0, j)),
        out_shape=jax.ShapeDtypeStruct((depth, nb, n), F32),
        compiler_params=_params(2, independent=True),
        name="modulation",
    )(cc, ada_w, ada_b.reshape(depth, 1, n))


def _ffn_kernel(*refs, alpha, mixer_out):
    if mixer_out:
        y_ref, wo_ref, g1_ref, b1_ref, x_ref, mod_ref, w1_ref, w2_ref, g_ref, b_ref, o_ref = refs
    else:
        x_ref, mod_ref, w1_ref, w2_ref, g_ref, b_ref, o_ref = refs
    n_hidden = w1_ref.shape[1]
    n_sub = x_ref.shape[0] // SUB_TILE

    def prologue(s):
        rows = slice(s * SUB_TILE, (s + 1) * SUB_TILE)
        x = x_ref[rows, :]
        if mixer_out:
            mix = jnp.dot(y_ref[rows, :], wo_ref[...], preferred_element_type=F32)
            x = _layernorm(alpha * x + mod_ref[2:3, :] * mix, g1_ref[...], b1_ref[...])
        return x, _modulate(x, mod_ref[3:4, :], mod_ref[4:5, :])

    def epilogue(s, x, acc):
        y = alpha * x + mod_ref[5:6, :] * acc
        o_ref[s * SUB_TILE:(s + 1) * SUB_TILE, :] = _layernorm(y, g_ref[...], b_ref[...])

    chunks = list(range(0, n_hidden, FFN_COL_CHUNK))
    nxt = prologue(0)
    done = None
    for s in range(n_sub):
        x, h = nxt
        acc = None
        for k, c0 in enumerate(chunks):
            z = jnp.dot(h, w1_ref[:, c0:c0 + FFN_COL_CHUNK], preferred_element_type=F32)
            a = jnp.maximum(z, 0.0)
            a = (a * a).astype(BF16)
            part = jnp.dot(a, w2_ref[c0:c0 + FFN_COL_CHUNK, :], preferred_element_type=F32)
            acc = part if acc is None else acc + part
            if k == 0 and done is not None:
                epilogue(*done)
            if k == len(chunks) // 2 and s + 1 < n_sub:
                nxt = prologue(s + 1)
        done = (s, x, acc)
    epilogue(*done)


def _ffn(x, mod, w1, w2, ln_g, ln_b, *, alpha, tm, mixer=None):
    rows, d = x.shape
    f = w1.shape[1]
    tm, tiles_per_row = _row_tiling(rows, mod.shape[0], tm)
    in_specs = [
        pl.BlockSpec((tm, d), lambda t: (t, 0)),
        _mod_spec(d, tiles_per_row),
        _const_spec((d, f)),
        _const_spec((f, d)),
        _const_spec((1, d)),
        _const_spec((1, d)),
    ]
    args = [x, mod, w1, w2, ln_g, ln_b]
    if mixer is not None:
        v_w = mixer[0].shape[1]
        in_specs = [pl.BlockSpec((tm, v_w), lambda t: (t, 0)), _const_spec((v_w, d)),
                    _const_spec((1, d)), _const_spec((1, d))] + in_specs
        args = list(mixer) + args
    return pl.pallas_call(
        functools.partial(_ffn_kernel, alpha=alpha, mixer_out=mixer is not None),
        grid=(rows // tm,),
        in_specs=in_specs,
        out_specs=pl.BlockSpec((tm, d), lambda t: (t, 0)),
        out_shape=jax.ShapeDtypeStruct((rows, d), F32),
        compiler_params=_params(1, independent=True),
        name="ffn" if mixer is None else "mix_ffn",
    )(*args)


def _gmlp_kernel(x_ref, mod_ref, win_ref, bin_ref, lng_ref, lnb_ref, ws_ref, bs_ref, wout_ref,
                 g_ref, b_ref, o_ref, s_ref, vbuf_ref, *, alpha):
    d_a = wout_ref.shape[0]
    n_groups, chunk, _ = ws_ref.shape
    cg = d_a // n_groups
    SUB_TILE = GMLP_SUB_TILE
    n_sub = x_ref.shape[0] // SUB_TILE

    def modulated(s):
        x = x_ref[s * SUB_TILE:(s + 1) * SUB_TILE, :]
        return x, _modulate(x, mod_ref[0:1, :], mod_ref[1:2, :])

    def v_chunk(h, k):
        lo = k * GMLP_COL_CHUNK
        cols = slice(d_a + lo, d_a + lo + GMLP_COL_CHUNK)
        act = _gelu(jnp.dot(h, win_ref[:, cols], preferred_element_type=F32) + bin_ref[:, cols])
        vbuf_ref[:, lo:lo + GMLP_COL_CHUNK] = act
        return jnp.sum(act, axis=-1, keepdims=True), jnp.sum(act * act, axis=-1, keepdims=True)

    def spatial_stage(s, stats):
        total = sum(st[0] for st in stats)
        total_sq = sum(st[1] for st in stats)
        mean = total * (1.0 / d_a)
        var = total_sq * (1.0 / d_a) - mean * mean
        rstd = lax.rsqrt(var + LN_EPS)
        t0 = s * SUB_TILE
        for g in range(n_groups):
            cols = slice(g * cg, (g + 1) * cg)
            v = ((vbuf_ref[:, cols] - mean) * rstd * lng_ref[:, cols] + lnb_ref[:, cols]).astype(BF16)
            for r0 in range(0, SUB_TILE, chunk):
                s_ref[t0 + r0:t0 + r0 + chunk, cols] = (
                    jnp.dot(ws_ref[g], v[r0:r0 + chunk, :], preferred_element_type=F32) + bs_ref[:, cols])

    def gated_chunk(s, h, k):
        cols = slice(k * GMLP_COL_CHUNK, (k + 1) * GMLP_COL_CHUNK)
        u = _gelu(jnp.dot(h, win_ref[:, cols], preferred_element_type=F32) + bin_ref[:, cols])
        return (u * s_ref[s * SUB_TILE:(s + 1) * SUB_TILE, cols]).astype(BF16)

    def out_chunk(acc, gated, k):
        rows = slice(k * GMLP_COL_CHUNK, (k + 1) * GMLP_COL_CHUNK)
        part = jnp.dot(gated, wout_ref[rows, :], preferred_element_type=F32)
        return part if acc is None else acc + part

    def epilogue(s, x, acc):
        y = alpha * x + mod_ref[2:3, :] * acc
        o_ref[s * SUB_TILE:(s + 1) * SUB_TILE, :] = _layernorm(y, g_ref[...], b_ref[...])

    n_chunks = d_a // GMLP_COL_CHUNK
    x, h = modulated(0)
    spatial_stage(0, [v_chunk(h, k) for k in range(n_chunks)])
    for s in range(n_sub):
        more = s + 1 < n_sub
        if more:
            x_next, h_next = modulated(s + 1)
        stats = []
        acc = None
        gated = gated_chunk(s, h, 0)
        for k in range(1, n_chunks):
            if more:
                stats.append(v_chunk(h_next, k - 1))
            gated_prev = gated
            gated = gated_chunk(s, h, k)
            acc = out_chunk(acc, gated_prev, k - 1)
        if more:
            stats.append(v_chunk(h_next, n_chunks - 1))
        acc = out_chunk(acc, gated, n_chunks - 1)
        if more:
            spatial_stage(s + 1, stats)
        epilogue(s, x, acc)
        if more:
            x, h = x_next, h_next


def _gmlp(x, mod, w_in, b_in, ln_g, ln_b, w_s, b_s_rows, w_out, ln1_g, ln1_b, *, alpha, tm):
    rows, d = x.shape
    d_a = w_out.shape[0]
    n_groups, chunk, _ = w_s.shape
    tm, tiles_per_row = _row_tiling(rows, mod.shape[0], tm)
    return pl.pallas_call(
        functools.partial(_gmlp_kernel, alpha=alpha),
        grid=(rows // tm,),
        in_specs=[
            pl.BlockSpec((tm, d), lambda t: (t, 0)),
            _mod_spec(d, tiles_per_row),
            _const_spec((d, 2 * d_a)),
            _const_spec((1, 2 * d_a)),
            _const_spec((1, d_a)),
            _const_spec((1, d_a)),
            _const_spec((n_groups, chunk, chunk)),
            _const_spec((chunk, d_a)),
            _const_spec((d_a, d)),
            _const_spec((1, d)),
            _const_spec((1, d)),
        ],
        out_specs=pl.BlockSpec((tm, d), lambda t: (t, 0)),
        out_shape=jax.ShapeDtypeStruct((rows, d), F32),
        scratch_shapes=[pltpu.VMEM((tm, d_a), F32),
                        pltpu.VMEM((GMLP_SUB_TILE, d_a), F32)],
        compiler_params=_params(1, independent=True),
        name="gmlp",
    )(x, mod, w_in, b_in, ln_g, ln_b, w_s, b_s_rows, w_out, ln1_g, ln1_b)


def _rope_pairs(t1, t2, cos, sin):
    return t1 * cos - t2 * sin, t1 * sin + t2 * cos


def _ret_proj_kv_kernel(x_ref, mod_ref, wkt_ref, wv_ref, kt_ref, v_ref, *, n_heads):
    n_sub, qk_w, c = kt_ref.shape
    scale = (qk_w // n_heads) ** -0.5
    for s in range(n_sub):
        rows = slice(s * c, (s + 1) * c)
        h = _modulate(x_ref[rows, :], mod_ref[0:1, :], mod_ref[1:2, :])
        kt = lax.dot_general(wkt_ref[...], h, (((1,), (1,)), ((), ())), preferred_element_type=F32)
        kt_ref[s] = (kt * scale).astype(BF16)
        v_ref[rows, :] = jnp.dot(h, wv_ref[...], preferred_element_type=F32).astype(BF16)


def _ret_proj_kernel(*refs, rope, n_heads):
    if rope:
        (x_ref, mod_ref, wq_ref, wkt_ref, wr_ref, cos_ref, sin_ref, cost_ref, sint_ref,
         q_ref, kt_ref, v_ref, gf_ref, gb_ref) = refs
    else:
        x_ref, mod_ref, wq_ref, wkt_ref, wr_ref, q_ref, kt_ref, v_ref, gf_ref, gb_ref = refs
    qk_w = wq_ref.shape[1]
    dk = qk_w // n_heads
    half = dk // 2
    v_w = v_ref.shape[1]
    n_sub, _, c = kt_ref.shape

    def prologue(s):
        return _modulate(x_ref[s * c:(s + 1) * c, :], mod_ref[0:1, :], mod_ref[1:2, :])

    h = prologue(0)
    for s in range(n_sub):
        rows = slice(s * c, (s + 1) * c)
        q = jnp.dot(h, wq_ref[...], preferred_element_type=F32)
        kt = lax.dot_general(wkt_ref[...], h, (((1,), (1,)), ((), ())), preferred_element_type=F32)
        kt = kt * (dk ** -0.5)
        if rope:
            cos, sin = cos_ref[rows, :], sin_ref[rows, :]
            cost, sint = cost_ref[:, rows], sint_ref[:, rows]
            for hh in range(n_heads):
                a, b = hh * dk, hh * dk + half
                o1, o2 = _rope_pairs(q[:, a:b], q[:, b:b + half], cos, sin)
                q_ref[rows, a:b] = o1.astype(BF16)
                q_ref[rows, b:b + half] = o2.astype(BF16)
                o1, o2 = _rope_pairs(kt[a:b, :], kt[b:b + half, :], cost, sint)
                kt_ref[s, a:b, :] = o1.astype(BF16)
                kt_ref[s, b:b + half, :] = o2.astype(BF16)
        else:
            q_ref[rows, :] = q.astype(BF16)
            kt_ref[s] = kt.astype(BF16)
        v_ref[rows, :] = jnp.dot(h, wr_ref[:, :v_w], preferred_element_type=F32).astype(BF16)
        h_next = prologue(s + 1) if s + 1 < n_sub else None
        for j, ref in enumerate((gf_ref, gb_ref), start=1):
            gate = jnp.dot(h, wr_ref[:, j * v_w:(j + 1) * v_w], preferred_element_type=F32)
            ref[rows, :] = (gate * jax.nn.sigmoid(gate)).astype(BF16)
        h = h_next


def _ret_proj_kv(x, mod, wkt, wr, *, n_heads):
    rows, d = x.shape
    c = RET_CHUNK
    qk_w = wkt.shape[0]
    v_w = wr.shape[1] // 3
    tm, tiles_per_row = _row_tiling(rows, mod.shape[0], PROJ_ROW_TILE)
    kt, v = pl.pallas_call(
        functools.partial(_ret_proj_kv_kernel, n_heads=n_heads),
        grid=(rows // tm,),
        in_specs=[
            pl.BlockSpec((tm, d), lambda t: (t, 0)),
            _mod_spec(d, tiles_per_row),
            _const_spec((qk_w, d)),
            pl.BlockSpec((d, v_w), lambda t: (0, 0), pipeline_mode=pl.Buffered(1)),
        ],
        out_specs=[pl.BlockSpec((tm // c, qk_w, c), lambda t: (t, 0, 0)), pl.BlockSpec((tm, v_w), lambda t: (t, 0))],
        out_shape=[jax.ShapeDtypeStruct((rows // c, qk_w, c), BF16), jax.ShapeDtypeStruct((rows, v_w), BF16)],
        compiler_params=_params(1, independent=True),
        name="ret_proj_kv",
    )(x, mod, wkt, wr)
    return None, kt, v, None, None


def _ret_proj(x, mod, wq, wkt, wr, rope_tabs, *, n_batch, n_heads):
    rows, d = x.shape
    c = RET_CHUNK
    seq = rows // n_batch
    qk_w = wq.shape[1]
    v_w = wr.shape[1] // 3
    rope = rope_tabs is not None
    tm, tiles_per_row = _row_tiling(rows, n_batch if rope else mod.shape[0], PROJ_ROW_TILE)
    if rope:
        tiles_per_row *= n_batch // mod.shape[0]
    assert tm % c == 0
    tiles_per_seq = seq // tm
    in_specs = [
        pl.BlockSpec((tm, d), lambda t: (t, 0)),
        _mod_spec(d, tiles_per_row),
        _const_spec((d, qk_w)),
        _const_spec((qk_w, d)),
        _const_spec((d, 3 * v_w)),
    ]
    args = [x, mod, wq, wkt, wr]
    if rope:
        half = qk_w // n_heads // 2
        in_specs += [
            pl.BlockSpec((tm, half), lambda t: (t % tiles_per_seq, 0)),
            pl.BlockSpec((tm, half), lambda t: (t % tiles_per_seq, 0)),
            pl.BlockSpec((half, tm), lambda t: (0, t % tiles_per_seq)),
            pl.BlockSpec((half, tm), lambda t: (0, t % tiles_per_seq)),
        ]
        args += list(rope_tabs)
    row_spec = lambda w: pl.BlockSpec((tm, w), lambda t: (t, 0))
    return pl.pallas_call(
        functools.partial(_ret_proj_kernel, rope=rope, n_heads=n_heads),
        grid=(rows // tm,),
        in_specs=in_specs,
        out_specs=[
            row_spec(qk_w),
            pl.BlockSpec((tm // c, qk_w, c), lambda t: (t, 0, 0)),
            row_spec(v_w), row_spec(v_w), row_spec(v_w),
        ],
        out_shape=[
            jax.ShapeDtypeStruct((rows, qk_w), BF16),
            jax.ShapeDtypeStruct((rows // c, qk_w, c), BF16),
            jax.ShapeDtypeStruct((rows, v_w), BF16),
            jax.ShapeDtypeStruct((rows, v_w), BF16),
            jax.ShapeDtypeStruct((rows, v_w), BF16),
        ],
        compiler_params=_params(1, independent=True),
        name="ret_proj_rope" if rope else "ret_proj",
    )(*args)


def _ret_scan_kernel(*refs, ctx_out):
    if ctx_out:
        (dec_ref, qc_ref, kc_ref, vc_ref, gfc_ref, gbc_ref, ql_ref, kl_ref, vl_ref, gfl_ref, gbl_ref,
         yl_ref, yc_ref) = refs[:13]
        scratch = refs[13:]
    else:
        dec_ref, kc_ref, vc_ref, ql_ref, kl_ref, vl_ref, gfl_ref, gbl_ref, yl_ref = refs[:9]
        qc_ref = gfc_ref = gbc_ref = yc_ref = None
        scratch = refs[9:]
    mask_ref, xi_ref, zeta_ref, s_ref, yacc_ref = scratch
    n_c, dk, c = kc_ref.shape
    n_l = kl_ref.shape[0]
    lc = n_c * c

    def log_gamma():
        return jnp.log1p(-jnp.exp(dec_ref[...]))

    @pl.when(pl.program_id(1) == 0)
    def _():
        lg = log_gamma()
        lgf, lgb = lg[0:1, :], lg[1:2, :]
        ii = lax.broadcasted_iota(jnp.int32, (c, c), 0).astype(F32)
        jj = lax.broadcasted_iota(jnp.int32, (c, c), 1).astype(F32)
        diff = ii - jj
        mask_ref[0] = jnp.where(diff >= 0, jnp.exp(lgf * jnp.maximum(diff, 0.0)), 0.0)
        mask_ref[1] = jnp.where(diff <= 0, jnp.exp(lgb * jnp.maximum(-diff, 0.0)), 0.0)
        pi = lax.broadcasted_iota(jnp.int32, (c, dk), 0).astype(F32)
        xi_ref[0] = jnp.exp(lgf * (pi + 1.0)).astype(BF16)
        xi_ref[1] = jnp.exp(lgb * (c - pi)).astype(BF16)
        pj = lax.broadcasted_iota(jnp.int32, (dk, c), 1).astype(F32)
        zeta_ref[0] = jnp.exp(lgf * (c - 1.0 - pj)).astype(BF16)
        zeta_ref[1] = jnp.exp(lgb * pj).astype(BF16)

    gamma_c = jnp.exp(log_gamma() * float(c))

    def chunk_step(q_ref, k_ref, v_ref, g_ref, y_ref, ci, d, row_off, emit):
        r0 = ci * c if isinstance(ci, int) else pl.multiple_of(ci * c, c)
        kt = k_ref[ci]
        vc = v_ref[pl.ds(r0, c), :]
        state = s_ref[d]
        kz = kt * zeta_ref[d]
        s_ref[d] = state * gamma_c[d:d + 1, :] + jnp.dot(kz, vc, preferred_element_type=F32)
        if emit is None:
            return
        qc = q_ref[pl.ds(r0, c), :]
        scores = jnp.dot(qc, kt, preferred_element_type=F32)
        p = (scores * mask_ref[d]).astype(BF16)
        qx = qc * xi_ref[d]
        o = (jnp.dot(p, vc, preferred_element_type=F32)
             + jnp.dot(qx, state.astype(BF16), preferred_element_type=F32))
        mu = jnp.mean(o, axis=-1, keepdims=True)
        oc = o - mu
        var = jnp.mean(oc * oc, axis=-1, keepdims=True)
        gated = (oc * lax.rsqrt(var + LN_EPS)).astype(BF16) * g_ref[pl.ds(r0, c), :]
        parked = pl.ds(row_off + r0, c)
        if emit == "first":
            yacc_ref[parked, :] = gated
        else:
            y_ref[pl.ds(r0, c), :] = yacc_ref[parked, :] + gated

    s_ref[...] = jnp.zeros_like(s_ref)
    seen = set()
    for ci in range(n_c):
        for d, g_ref, cj in ((0, gfc_ref, ci), (1, gbc_ref, n_c - 1 - ci)):
            emit = None if not ctx_out else ("second" if cj in seen else "first")
            seen.add(cj)
            chunk_step(qc_ref, kc_ref, vc_ref, g_ref, yc_ref, cj, d, 0, emit)

    def pair(emit):
        def body(i, carry):
            chunk_step(ql_ref, kl_ref, vl_ref, gfl_ref, yl_ref, i, 0, lc, emit)
            chunk_step(ql_ref, kl_ref, vl_ref, gbl_ref, yl_ref, n_l - 1 - i, 1, lc, emit)
            return carry
        return body

    lax.fori_loop(0, n_l // 2, pair("first"), 0, unroll=SCAN_UNROLL)
    lax.fori_loop(n_l // 2, n_l, pair("second"), 0, unroll=SCAN_UNROLL)


def _ret_scan(decay, ctx_parts, lat_parts, *, n_batch, n_heads, ctx_out):
    qc, kc, vc, gfc, gbc = ctx_parts
    ql, kl, vl, gfl, gbl = lat_parts
    _, qk_w, c = kc.shape
    n_c, n_l = kc.shape[0] // n_batch, kl.shape[0] // n_batch
    assert n_l % 2 == 0, "the latent walk is split into a first and a second half"
    dk = qk_w // n_heads
    dv = vl.shape[1] // n_heads
    lc, ll = n_c * c, n_l * c

    def seq_specs(seq, n_chunks, kv_only=False):
        row = lambda w: pl.BlockSpec((seq, w), lambda h, b: (b, h))
        kt_spec = pl.BlockSpec((n_chunks, dk, c), lambda h, b: (b, h, 0))
        return [kt_spec, row(dv)] if kv_only else [row(dk), kt_spec, row(dv), row(dv), row(dv)]

    out_specs = [pl.BlockSpec((ll, dv), lambda h, b: (b, h))]
    out_shape = [jax.ShapeDtypeStruct((n_batch * ll, n_heads * dv), BF16)]
    if ctx_out:
        out_specs.append(pl.BlockSpec((lc, dv), lambda h, b: (b, h)))
        out_shape.append(jax.ShapeDtypeStruct((n_batch * lc, n_heads * dv), BF16))
    res = pl.pallas_call(
        functools.partial(_ret_scan_kernel, ctx_out=ctx_out),
        grid=(n_heads, n_batch),
        in_specs=([pl.BlockSpec((None, 2, 1), lambda h, b: (h, 0, 0))]
                  + seq_specs(lc, n_c, kv_only=not ctx_out) + seq_specs(ll, n_l)),
        out_specs=out_specs,
        out_shape=out_shape,
        scratch_shapes=[
            pltpu.VMEM((2, c, c), F32),
            pltpu.VMEM((2, c, dk), BF16),
            pltpu.VMEM((2, dk, c), BF16),
            pltpu.VMEM((2, dk, dv), F32),
            pltpu.VMEM((lc + ll, dv), BF16),
        ],
        compiler_params=_params(2),
        name="ret_scan",
    )(decay, *((qc, kc, vc, gfc, gbc) if ctx_out else (kc, vc)), ql, kl, vl, gfl, gbl)
    return res if ctx_out else (res[0], None)


def _rope_tables(seq, dk):
    t = jnp.arange(seq)
    row = (t // GRID_W).astype(F32)
    col = (t % GRID_W).astype(F32)
    n_freq = dk // 4
    inv = ROPE_BASE ** (-jnp.arange(n_freq, dtype=F32) / n_freq)
    ang = jnp.concatenate([row[:, None] * inv, col[:, None] * inv], -1)
    cos, sin = jnp.cos(ang), jnp.sin(ang)
    return cos, sin, cos.T, sin.T


def kernel(x, c, ctx, c_ctx, ada_w, ada_b, ln1_g, ln1_b, ln2_g, ln2_b, ffn_w1, ffn_w2,
           a_w_in, a_b_in, a_ln_g, a_ln_b, a_w_s, a_b_s, a_w_out, b_w_in, b_decay, b_w_out):
    n_batch, seq, d = x.shape
    ctx_len = ctx.shape[1]
    depth = ada_w.shape[0]
    alpha = float((2 * depth) ** 0.25)
    n_heads = b_decay.shape[-1]
    dk = d // n_heads
    qk_w = n_heads * dk
    d_a = a_w_out.shape[1]
    n_groups = a_w_s.shape[1]

    mod = _modulation(jnp.concatenate([c, c_ctx[None]], axis=0), ada_w, ada_b)
    mod = mod.reshape(depth, n_batch + 1, 6, d)
    rope_tabs = _rope_tables(seq, dk)

    xl = x.reshape(n_batch * seq, d)
    xc = ctx.reshape(n_batch * ctx_len, d)
    row = lambda p: p.reshape(1, -1)

    for i in range(depth):
        ctx_out = i < depth - 1
        mod_l, mod_c = mod[i, :n_batch], mod[i, n_batch:]
        j = i // 2
        g1, b1 = row(ln1_g[i]), row(ln1_b[i])
        if i % 2 == 0:
            bs_rows = jnp.repeat(a_b_s[j].T, d_a // n_groups, axis=1)
            weights = (a_w_in[j].astype(BF16), row(a_b_in[j]), row(a_ln_g[j]), row(a_ln_b[j]),
                       a_w_s[j].astype(BF16), bs_rows, a_w_out[j].astype(BF16), g1, b1)
            xl = _gmlp(xl, mod_l, *weights, alpha=alpha, tm=GMLP_ROW_TILE)
            if ctx_out:
                xc = _gmlp(xc, mod_c, *weights, alpha=alpha, tm=GMLP_ROW_TILE)
            mix_l = mix_c = None
        else:
            w_in = b_w_in[j]
            wq = w_in[:, :qk_w].astype(BF16)
            wkt = w_in[:, qk_w:2 * qk_w].T.astype(BF16)
            wr = w_in[:, 2 * qk_w:].astype(BF16)
            w_out = b_w_out[j].astype(BF16)
            lat = _ret_proj(xl, mod_l, wq, wkt, wr, rope_tabs, n_batch=n_batch, n_heads=n_heads)
            if ctx_out:
                cpt = _ret_proj(xc, mod_c, wq, wkt, wr, None, n_batch=n_batch, n_heads=n_heads)
            else:
                cpt = _ret_proj_kv(xc, mod_c, wkt, wr, n_heads=n_heads)
            decay = b_decay[j].T.reshape(n_heads, 2, 1)
            y_l, y_c = _ret_scan(decay, cpt, lat, n_batch=n_batch, n_heads=n_heads, ctx_out=ctx_out)
            mix_l, mix_c = (y_l, w_out, g1, b1), (y_c, w_out, g1, b1)
        ffn_args = (ffn_w1[i].astype(BF16), ffn_w2[i].astype(BF16), row(ln2_g[i]), row(ln2_b[i]))
        xl = _ffn(xl, mod_l, *ffn_args, alpha=alpha, tm=FFN_ROW_TILE, mixer=mix_l)
        if ctx_out:
            xc = _ffn(xc, mod_c, *ffn_args, alpha=alpha, tm=FFN_ROW_TILE, mixer=mix_c)
    return xl.reshape(n_batch, seq, d)
```
